```python
import jax, jax.numpy as jnp
from jax import lax
import numpy as np

D_MODEL = 2048
BATCH = 2
SEQ = 4096
DEPTH = 4
DEC_BATCH = 2
DEC_SEQ = 16384
PAST_LEN = 128

M_HEADS = 4
M_HEAD_DIM = D_MODEL // 8
M_WIDTH = M_HEADS * M_HEAD_DIM
M_CHUNK = 128
M_FGATE_BIAS_LO = 3.0
M_FGATE_BIAS_HI = 6.0
G_HEADS = 4
G_VAL_DIM = D_MODEL // 8
G_KEY_DIM = G_VAL_DIM // 2
G_VWIDTH = G_HEADS * G_VAL_DIM
G_KWIDTH = G_HEADS * G_KEY_DIM
G_LOWRANK = 16
G_GATE_NORM = 16.0
G_CHUNK = 32
D_FF = -(-8 * D_MODEL // (3 * 256)) * 256
EPS = 1e-6
NEG = -1e30

IN_SIZES = (M_WIDTH, M_WIDTH, M_WIDTH, M_WIDTH, 2 * M_HEADS, 2 * M_HEADS,
            G_KWIDTH, G_KWIDTH, G_VWIDTH, G_VWIDTH, 2 * G_LOWRANK,
            D_MODEL, D_MODEL)
IN_WIDTH = sum(IN_SIZES)
M_F_OFFSET = 4 * M_WIDTH + 2 * M_HEADS

kernel_name = "bidir_mlstm_gla_gated_merge_encoder"


def rmsnorm(x, g):
    xf = x.astype(jnp.float32)
    y = xf * lax.rsqrt(jnp.mean(xf * xf, axis=-1, keepdims=True) + EPS)
    return (y * g.astype(jnp.float32)).astype(x.dtype)


def head_rmsnorm(h, g, n_heads):
    B, S, H, d = h.shape
    h = h * lax.rsqrt(jnp.mean(h * h, axis=-1, keepdims=True) + EPS)
    return h.reshape(B, S, H * d) * g.astype(jnp.float32)


def split_cols(z):
    outs, o = [], 0
    for s in IN_SIZES:
        outs.append(z[..., o:o + s])
        o += s
    return outs


def to_chunks(t, L):
    B, S, H, d = t.shape
    return t.reshape(B, S // L, L, H, d).transpose(1, 0, 3, 2, 4)


def from_chunks(t):
    N, B, H, L, d = t.shape
    return t.transpose(1, 0, 3, 2, 4).reshape(B, N * L, H, d)


def flip(t):
    return jnp.flip(t, axis=1)


def mlstm_chunkwise(q, k, v, i_pre, f_pre):
    B, S, H, dk = q.shape
    dv = v.shape[-1]
    L = M_CHUNK
    qc = to_chunks(q * dk ** -0.5, L)
    kc, vc = to_chunks(k, L), to_chunks(v, L)
    ic = to_chunks(i_pre[..., None], L)[..., 0]
    lfc = to_chunks(jax.nn.log_sigmoid(f_pre)[..., None], L)[..., 0]
    causal = jnp.tril(jnp.ones((L, L), dtype=bool))

    def step(carry, xs):
        C, n, m = carry
        qj, kj, vj, ij, lf = xs
        a = jnp.cumsum(lf, axis=-1)
        A = a[..., -1]
        Dm = jnp.where(causal, a[..., :, None] - a[..., None, :] + ij[..., None, :], NEG)
        inter = a + m[..., None]
        m_row = jnp.maximum(inter, jnp.max(Dm, axis=-1))
        s = jnp.einsum('bhld,bhsd->bhls', qj, kj) * jnp.exp(Dm - m_row[..., None])
        e_inter = jnp.exp(inter - m_row)
        num = jnp.einsum('bhls,bhsv->bhlv', s, vj) + e_inter[..., None] * jnp.einsum('bhld,bhdv->bhlv', qj, C)
        den = jnp.sum(s, axis=-1) + e_inter * jnp.einsum('bhld,bhd->bhl', qj, n)
        h = num / jnp.maximum(jnp.abs(den), jnp.exp(-m_row))[..., None]
        g = A[..., None] - a + ij
        m_new = jnp.maximum(A + m, jnp.max(g, axis=-1))
        kw = kj * jnp.exp(g - m_new[..., None])[..., None]
        decay = jnp.exp(A + m - m_new)
        C = decay[..., None, None] * C + jnp.einsum('bhsd,bhsv->bhdv', kw, vj)
        n = decay[..., None] * n + jnp.sum(kw, axis=-2)
        return (C, n, m_new), h

    init = (jnp.zeros((B, H, dk, dv), jnp.float32), jnp.zeros((B, H, dk), jnp.float32),
            jnp.full((B, H), NEG, jnp.float32))
    _, hs = lax.scan(step, init, (qc, kc, vc, ic, lfc))
    return from_chunks(hs)


def gla_chunked(q, k, v, log_a):
    B, S, H, dk = q.shape
    dv = v.shape[-1]
    L = G_CHUNK
    qc = to_chunks(q * dk ** -0.5, L)
    kc, vc, lac = to_chunks(k, L), to_chunks(v, L), to_chunks(log_a, L)
    causal = jnp.tril(jnp.ones((L, L), dtype=bool))[..., None]

    def step(St, xs):
        qj, kj, vj, la = xs
        G = jnp.cumsum(la, axis=-2)
        rel = jnp.exp(jnp.where(causal, G[..., :, None, :] - G[..., None, :, :], NEG))
        att = jnp.einsum('bhld,bhsd,bhlsd->bhls', qj, kj, rel)
        o = jnp.einsum('bhls,bhsv->bhlv', att, vj) + jnp.einsum('bhld,bhdv->bhlv', qj * jnp.exp(G), St)
        G_last = G[..., -1, :]
        kw = kj * jnp.exp(G_last[..., None, :] - G)
        St = jnp.exp(G_last)[..., None] * St + jnp.einsum('bhsd,bhsv->bhdv', kw, vj)
        return St, o

    _, os_ = lax.scan(step, jnp.zeros((B, H, dk, dv), jnp.float32), (qc, kc, vc, lac))
    return from_chunks(os_)


def hybrid_layer(x, ln1, w_in, b_in, m_norm, w_lr2, b_lr2, g_norm, w_pa, w_pb, w_o, ln2, w_gu, w_down):
    B, S, _ = x.shape
    dt = x.dtype
    xn = rmsnorm(x, ln1)
    z = (xn @ w_in + b_in).astype(jnp.float32)
    mq, mk, mv, mo, mi, mf, gq, gk, gv, gg, glr, ga, gb = split_cols(z)

    mq = mq.reshape(B, S, M_HEADS, M_HEAD_DIM)
    mk = mk.reshape(B, S, M_HEADS, M_HEAD_DIM)
    mv = mv.reshape(B, S, M_HEADS, M_HEAD_DIM)
    mi = mi.reshape(B, S, 2, M_HEADS)
    mf = mf.reshape(B, S, 2, M_HEADS)
    hA = (mlstm_chunkwise(mq, mk, mv, mi[:, :, 0], mf[:, :, 0])
          + flip(mlstm_chunkwise(flip(mq), flip(mk), flip(mv), flip(mi[:, :, 1]), flip(mf[:, :, 1]))))
    hA = head_rmsnorm(hA, m_norm, M_HEADS) * jax.nn.sigmoid(mo)

    gq = gq.reshape(B, S, G_HEADS, G_KEY_DIM)
    gk = gk.reshape(B, S, G_HEADS, G_KEY_DIM)
    gv = gv.reshape(B, S, G_HEADS, G_VAL_DIM)
    glr = glr.reshape(B, S, 2, G_LOWRANK)
    log_a = jax.nn.log_sigmoid(jnp.einsum('bsnr,nrk->bsnk', glr, w_lr2.astype(jnp.float32))
                               + b_lr2.astype(jnp.float32)) / G_GATE_NORM
    la_f = log_a[:, :, 0].reshape(B, S, G_HEADS, G_KEY_DIM)
    la_b = log_a[:, :, 1].reshape(B, S, G_HEADS, G_KEY_DIM)
    hB = gla_chunked(gq, gk, gv, la_f) + flip(gla_chunked(flip(gq), flip(gk), flip(gv), flip(la_b)))
    hB = head_rmsnorm(hB, g_norm, G_HEADS) * jax.nn.silu(gg)

    merged = (jax.nn.sigmoid(ga).astype(dt) * (hA.astype(dt) @ w_pa)
              + jax.nn.sigmoid(gb).astype(dt) * (hB.astype(dt) @ w_pb))
    x = x + merged.astype(dt) @ w_o

    xn = rmsnorm(x, ln2)
    gate, up = jnp.split(xn @ w_gu, 2, axis=-1)
    return x + (jax.nn.silu(gate) * up) @ w_down


def trunk(x, ln1, w_in, b_in, m_norm, w_lr2, b_lr2, g_norm, w_pa, w_pb, w_o, ln2, w_gu, w_down, ln_f):
    for l in range(DEPTH):
        x = hybrid_layer(x, ln1[l], w_in[l], b_in[l], m_norm[l], w_lr2[l], b_lr2[l], g_norm[l],
                         w_pa[l], w_pb[l], w_o[l], ln2[l], w_gu[l], w_down[l])
    return rmsnorm(x, ln_f)


def setup_inputs(seed: int = 0) -> dict:
    key = jax.random.key(seed)
    ks = jax.random.split(key, 16)
    f32 = jnp.float32

    def nrm(k, shape, scale):
        return jax.random.normal(k, shape, f32) * scale

    def gain(k, shape):
        return 1.0 + 0.02 * jax.random.normal(k, shape, f32)

    b_in = nrm(ks[3], (DEPTH, IN_WIDTH), 0.01)
    f_bias = jnp.tile(jnp.linspace(M_FGATE_BIAS_LO, M_FGATE_BIAS_HI, M_HEADS, dtype=f32), 2)
    b_in = b_in.at[:, M_F_OFFSET:M_F_OFFSET + 2 * M_HEADS].add(f_bias)
    return {
        "x_prompt": jax.random.normal(ks[0], (BATCH, SEQ, D_MODEL), f32),
        "x_sample": jax.random.normal(ks[1], (DEC_BATCH, DEC_SEQ, D_MODEL), f32),
        "ln1": gain(ks[2], (DEPTH, D_MODEL)),
        "w_in": nrm(ks[4], (DEPTH, D_MODEL, IN_WIDTH), D_MODEL ** -0.5),
        "b_in": b_in,
        "m_norm": gain(ks[5], (DEPTH, M_WIDTH)),
        "w_lr2": nrm(ks[6], (DEPTH, 2, G_LOWRANK, G_KWIDTH), G_LOWRANK ** -0.5),
        "b_lr2": nrm(ks[7], (DEPTH, 2, G_KWIDTH), 0.01),
        "g_norm": gain(ks[8], (DEPTH, G_VWIDTH)),
        "w_pa": nrm(ks[9], (DEPTH, M_WIDTH, D_MODEL), M_WIDTH ** -0.5),
        "w_pb": nrm(ks[10], (DEPTH, G_VWIDTH, D_MODEL), G_VWIDTH ** -0.5),
        "w_o": nrm(ks[11], (DEPTH, D_MODEL, D_MODEL), D_MODEL ** -0.5),
        "ln2": gain(ks[12], (DEPTH, D_MODEL)),
        "w_gu": nrm(ks[13], (DEPTH, D_MODEL, 2 * D_FF), D_MODEL ** -0.5),
        "w_down": nrm(ks[14], (DEPTH, D_FF, D_MODEL), D_FF ** -0.5),
        "ln_f": gain(ks[15], (D_MODEL,)),
    }


def reference(x_prompt, x_sample, ln1, w_in, b_in, m_norm, w_lr2, b_lr2, g_norm, w_pa, w_pb, w_o,
              ln2, w_gu, w_down, ln_f):
    y_prompt = trunk(x_prompt, ln1, w_in, b_in, m_norm, w_lr2, b_lr2, g_norm, w_pa, w_pb, w_o,
                     ln2, w_gu, w_down, ln_f)
    y_sample = trunk(x_sample, ln1, w_in, b_in, m_norm, w_lr2, b_lr2, g_norm, w_pa, w_pb, w_o,
                     ln2, w_gu, w_down, ln_f)
    return (y_prompt, y_sample)
```

```python
import functools

import jax
import jax.numpy as jnp
from jax import lax
from jax.experimental import pallas as pl
from jax.experimental.pallas import tpu as pltpu

F32 = jnp.float32
BF16 = jnp.bfloat16

D_MODEL = 2048
DEPTH = 4
M_HEADS = 4
M_HEAD_DIM = 256
M_WIDTH = 1024
M_CHUNK = 128
G_HEADS = 4
G_VAL_DIM = 256
G_KEY_DIM = 128
G_VWIDTH = 1024
G_KWIDTH = 512
G_LOWRANK = 16
G_GATE_NORM = 16.0
G_CHUNK = 32
D_FF = 5632
EPS = 1e-6
NEG = -1e30

BIG_WIDTH = 4 * M_WIDTH + 2 * G_KWIDTH + 2 * G_VWIDTH + 2 * D_MODEL
SMALL_WIDTH = 128
ROW_BLOCK = 128

VMEM_LIMIT = 56 * 1024 * 1024


def _cparams(sem):
    return pltpu.CompilerParams(dimension_semantics=sem, vmem_limit_bytes=VMEM_LIMIT)


def _log_sigmoid(x):
    return jnp.minimum(x, 0.0) - jnp.log1p(jnp.exp(-jnp.abs(x)))


def _sigmoid(x):
    return 1.0 / (1.0 + jnp.exp(-x))


def _inproj_kernel(x_ref, g_ref, w_ref, b_ref, ws_ref, bs_ref, wst_ref, bst_ref,
                   z_ref, zs_ref, zst_ref, xn_ref):
    @pl.when(pl.program_id(1) == 0)
    def _():
        x = x_ref[...]
        y = x * lax.rsqrt(jnp.mean(x * x, axis=-1, keepdims=True) + EPS) * g_ref[...]
        xn = y.astype(BF16)
        xn_ref[...] = xn
        zs_ref[...] = jnp.dot(xn, ws_ref[...], preferred_element_type=F32) + bs_ref[...]
        zst_ref[...] = lax.dot_general(wst_ref[...], xn, (((1,), (1,)), ((), ())),
                                       preferred_element_type=F32) + bst_ref[...]

    z_ref[...] = jnp.dot(xn_ref[...], w_ref[...], preferred_element_type=F32) + b_ref[...]


def _inproj(x, g, w, b, ws, bs, wst, bst, *, tm=1024, tn=1024):
    T = x.shape[0]
    N = w.shape[1]
    return pl.pallas_call(
        _inproj_kernel,
        grid=(T // tm, N // tn),
        in_specs=[
            pl.BlockSpec((tm, D_MODEL), lambda i, j: (i, 0), pipeline_mode=pl.Buffered(1)),
            pl.BlockSpec((1, D_MODEL), lambda i, j: (0, 0)),
            pl.BlockSpec((D_MODEL, tn), lambda i, j: (0, j)),
            pl.BlockSpec((1, tn), lambda i, j: (0, j)),
            pl.BlockSpec((D_MODEL, SMALL_WIDTH), lambda i, j: (0, 0)),
            pl.BlockSpec((1, SMALL_WIDTH), lambda i, j: (0, 0)),
            pl.BlockSpec((SMALL_WIDTH, D_MODEL), lambda i, j: (0, 0)),
            pl.BlockSpec((SMALL_WIDTH, 1), lambda i, j: (0, 0)),
        ],
        out_specs=[
            pl.BlockSpec((tm, tn), lambda i, j: (i, j)),
            pl.BlockSpec((tm, SMALL_WIDTH), lambda i, j: (i, 0)),
            pl.BlockSpec((SMALL_WIDTH, tm), lambda i, j: (0, i)),
        ],
        out_shape=[
            jax.ShapeDtypeStruct((T, N), F32),
            jax.ShapeDtypeStruct((T, SMALL_WIDTH), F32),
            jax.ShapeDtypeStruct((SMALL_WIDTH, T), F32),
        ],
        scratch_shapes=[pltpu.VMEM((tm, D_MODEL), BF16)],
        compiler_params=_cparams(("parallel", "arbitrary")),
        name="inproj",
    )(x, g, w, b, ws, bs, wst, bst)


def _merge_kernel(ha_ref, hb_ref, wpa_ref, wpb_ref, ga_ref, gb_ref, out_ref):
    pa = jnp.dot(ha_ref[...], wpa_ref[...], preferred_element_type=F32)
    pb = jnp.dot(hb_ref[...], wpb_ref[...], preferred_element_type=F32)
    out_ref[...] = (_sigmoid(ga_ref[...]) * pa + _sigmoid(gb_ref[...]) * pb).astype(BF16)


def _merge(ha, hb, wpa, wpb, z, *, tm=1024, tn=1024):
    T = ha.shape[0]
    ga_blk = (BIG_WIDTH - 2 * D_MODEL) // tn
    gb_blk = (BIG_WIDTH - D_MODEL) // tn
    return pl.pallas_call(
        _merge_kernel,
        grid=(T // tm, D_MODEL // tn),
        in_specs=[
            pl.BlockSpec((tm, M_WIDTH), lambda i, j: (i, 0)),
            pl.BlockSpec((tm, G_VWIDTH), lambda i, j: (i, 0)),
            pl.BlockSpec((M_WIDTH, tn), lambda i, j: (0, j)),
            pl.BlockSpec((G_VWIDTH, tn), lambda i, j: (0, j)),
            pl.BlockSpec((tm, tn), lambda i, j: (i, ga_blk + j)),
            pl.BlockSpec((tm, tn), lambda i, j: (i, gb_blk + j)),
        ],
        out_specs=pl.BlockSpec((tm, tn), lambda i, j: (i, j)),
        out_shape=jax.ShapeDtypeStruct((T, D_MODEL), BF16),
        compiler_params=_cparams(("parallel", "arbitrary")),
        name="merge",
    )(ha, hb, wpa, wpb, z, z)


def _resmm_kernel(x_ref, a_ref, w_ref, out_ref):
    out_ref[...] = x_ref[...] + jnp.dot(a_ref[...], w_ref[...], preferred_element_type=F32)


def _resmm(x, a, w, *, tm=1024, tn=512, name="resmm"):
    T, K = a.shape
    N = w.shape[1]
    return pl.pallas_call(
        _resmm_kernel,
        grid=(T // tm, N // tn),
        in_specs=[
            pl.BlockSpec((tm, tn), lambda i, j: (i, j)),
            pl.BlockSpec((tm, K), lambda i, j: (i, 0)),
            pl.BlockSpec((K, tn), lambda i, j: (0, j)),
        ],
        out_specs=pl.BlockSpec((tm, tn), lambda i, j: (i, j)),
        out_shape=jax.ShapeDtypeStruct((T, N), F32),
        compiler_params=_cparams(("parallel", "arbitrary")),
        name=name,
    )(x, a, w)


def _ffn_up_kernel(x_ref, g_ref, wg_ref, wu_ref, h_ref, xn_ref):
    @pl.when(pl.program_id(1) == 0)
    def _():
        x = x_ref[...]
        y = x * lax.rsqrt(jnp.mean(x * x, axis=-1, keepdims=True) + EPS) * g_ref[...]
        xn_ref[...] = y.astype(BF16)

    xn = xn_ref[...]
    gate = jnp.dot(xn, wg_ref[...], preferred_element_type=F32)
    up = jnp.dot(xn, wu_ref[...], preferred_element_type=F32)
    h_ref[...] = (gate * _sigmoid(gate) * up).astype(BF16)


def _ffn_up(x, g, wgu, *, tm=1024, tn=512):
    T = x.shape[0]
    nj = D_FF // tn
    return pl.pallas_call(
        _ffn_up_kernel,
        grid=(T // tm, nj),
        in_specs=[
            pl.BlockSpec((tm, D_MODEL), lambda i, j: (i, 0), pipeline_mode=pl.Buffered(1)),
            pl.BlockSpec((1, D_MODEL), lambda i, j: (0, 0)),
            pl.BlockSpec((D_MODEL, tn), lambda i, j: (0, j)),
            pl.BlockSpec((D_MODEL, tn), lambda i, j: (0, nj + j)),
        ],
        out_specs=pl.BlockSpec((tm, tn), lambda i, j: (i, j)),
        out_shape=jax.ShapeDtypeStruct((T, D_FF), BF16),
        scratch_shapes=[pltpu.VMEM((tm, D_MODEL), BF16)],
        compiler_params=_cparams(("parallel", "arbitrary")),
        name="ffn_up",
    )(x, g, wgu, wgu)


def _final_norm_kernel(x_ref, g_ref, out_ref):
    x = x_ref[...]
    out_ref[...] = x * lax.rsqrt(jnp.mean(x * x, axis=-1, keepdims=True) + EPS) * g_ref[...]


def _final_norm(x, g, *, tm=1024):
    T = x.shape[0]
    return pl.pallas_call(
        _final_norm_kernel,
        grid=(T // tm,),
        in_specs=[
            pl.BlockSpec((tm, D_MODEL), lambda i: (i, 0)),
            pl.BlockSpec((1, D_MODEL), lambda i: (0, 0)),
        ],
        out_specs=pl.BlockSpec((tm, D_MODEL), lambda i: (i, 0)),
        out_shape=jax.ShapeDtypeStruct((T, D_MODEL), F32),
        compiler_params=_cparams(("parallel",)),
        name="final_norm",
    )(x, g)


def _lane_scan(x, op, fill, reverse):
    n = x.shape[1]
    lane = lax.broadcasted_iota(jnp.int32, x.shape, 1)
    sh = 1
    while sh < n:
        if reverse:
            x = op(x, jnp.where(lane < n - sh, pltpu.roll(x, n - sh, axis=1), fill))
        else:
            x = op(x, jnp.where(lane >= sh, pltpu.roll(x, sh, axis=1), fill))
        sh *= 2
    return x


def _to_column(row, eye):
    return jnp.sum(jnp.where(eye, row, 0.0), axis=1, keepdims=True)


def _head_norm_gate(hsum, gain, gate):
    y = hsum * lax.rsqrt(jnp.mean(hsum * hsum, axis=-1, keepdims=True) + EPS)
    return (y * gain * gate).astype(BF16)


def _mlstm_kernel(q_ref, k_ref, v_ref, gt_ref, *rest, reverse):
    if reverse:
        hf_ref, mo_ref, gain_ref, out_ref, c_scr, n_scr, m_scr = rest
    else:
        out_ref, c_scr, n_scr, m_scr = rest
    L = ROW_BLOCK
    dh = M_HEAD_DIM

    @pl.when(pl.program_id(1) == 0)
    def _():
        c_scr[...] = jnp.zeros(c_scr.shape, F32)
        n_scr[...] = jnp.zeros(n_scr.shape, F32)
        m_scr[...] = jnp.full(m_scr.shape, NEG, F32)

    i_rows = gt_ref[0:8, :]
    lf = _log_sigmoid(gt_ref[8:16, :])
    a = _lane_scan(lf, jnp.add, 0.0, reverse)
    a_tot = jnp.broadcast_to(jnp.sum(lf, axis=1, keepdims=True), (8, L))
    m_old = m_scr[...]
    g = a_tot - a + i_rows
    m_new = jnp.maximum(a_tot + m_old, jnp.max(g, axis=1, keepdims=True))
    b = i_rows - a
    big_m = jnp.maximum(m_old, _lane_scan(b, jnp.maximum, -jnp.inf, reverse))
    e_int = jnp.exp(m_old - big_m)
    floor = jnp.exp(-(a + big_m))
    ksc = jnp.exp(g - m_new)
    decay = jnp.exp(a_tot + m_old - m_new)
    m_scr[...] = m_new

    rr = lax.broadcasted_iota(jnp.int32, (L, L), 0)
    cc = lax.broadcasted_iota(jnp.int32, (L, L), 1)
    eye = rr == cc
    mask = (cc >= rr) if reverse else (cc <= rr)
    base = M_HEADS if reverse else 0

    for h in range(M_HEADS):
        r = base + h
        sl = slice(h * dh, (h + 1) * dh)
        m_c = _to_column(big_m[r:r + 1, :], eye)
        e_c = _to_column(e_int[r:r + 1, :], eye)
        f_c = _to_column(floor[r:r + 1, :], eye)
        k_c = _to_column(ksc[r:r + 1, :], eye)
        dec = jnp.concatenate([decay[r:r + 1, :]] * (dh // L), axis=1)

        q = q_ref[:, sl] * (dh ** -0.5)
        k = k_ref[:, sl]
        v_b = v_ref[:, sl].astype(BF16)
        q_b = q.astype(BF16)
        c_old = c_scr[h]
        n_old = n_scr[h]

        qk = lax.dot_general(q_b, k.astype(BF16), (((1,), (1,)), ((), ())),
                             preferred_element_type=F32)
        p = jnp.exp(jnp.where(mask, b[r:r + 1, :] - m_c, NEG))
        s = qk * p
        qn = jnp.sum(q * n_old, axis=1, keepdims=True)
        den = jnp.sum(s, axis=1, keepdims=True) + e_c * qn
        num = (jnp.dot(s.astype(BF16), v_b, preferred_element_type=F32)
               + e_c * jnp.dot(q_b, c_old.astype(BF16), preferred_element_type=F32))
        hh = num / jnp.maximum(jnp.abs(den), f_c)

        kw = k * k_c
        c_scr[h] = dec * c_old + lax.dot_general(kw.astype(BF16), v_b, (((0,), (0,)), ((), ())),
                                                 preferred_element_type=F32)
        n_scr[h] = dec * n_old + jnp.sum(kw, axis=0, keepdims=True)

        if reverse:
            out_ref[:, sl] = _head_norm_gate(hf_ref[:, sl] + hh, gain_ref[:, sl],
                                             _sigmoid(mo_ref[:, sl]))
        else:
            out_ref[:, sl] = hh


def _mlstm(z, zst, row_off, batch, seq, *, reverse, hf=None, gain=None):
    L = ROW_BLOCK
    nblk = seq // L
    off = row_off // L

    def rows(bi, n):
        return off + bi * nblk + (nblk - 1 - n if reverse else n)

    in_specs = [
        pl.BlockSpec((L, M_WIDTH), lambda bi, n: (rows(bi, n), 0)),
        pl.BlockSpec((L, M_WIDTH), lambda bi, n: (rows(bi, n), 1)),
        pl.BlockSpec((L, M_WIDTH), lambda bi, n: (rows(bi, n), 2)),
        pl.BlockSpec((SMALL_WIDTH, L), lambda bi, n: (0, rows(bi, n))),
    ]
    args = [z, z, z, zst]
    if reverse:
        in_specs += [
            pl.BlockSpec((L, M_WIDTH), lambda bi, n: (rows(bi, n) - off, 0)),
            pl.BlockSpec((L, M_WIDTH), lambda bi, n: (rows(bi, n), 3)),
            pl.BlockSpec((1, M_WIDTH), lambda bi, n: (0, 0)),
        ]
        args += [hf, z, gain]
    return pl.pallas_call(
        functools.partial(_mlstm_kernel, reverse=reverse),
        grid=(batch, nblk),
        in_specs=in_specs,
        out_specs=pl.BlockSpec((L, M_WIDTH), lambda bi, n: (rows(bi, n) - off, 0)),
        out_shape=jax.ShapeDtypeStruct((batch * seq, M_WIDTH), BF16 if reverse else F32),
        scratch_shapes=[
            pltpu.VMEM((M_HEADS, M_HEAD_DIM, M_HEAD_DIM), F32),
            pltpu.VMEM((M_HEADS, 1, M_HEAD_DIM), F32),
            pltpu.VMEM((8, L), F32),
        ],
        compiler_params=_cparams(("parallel", "arbitrary")),
        name="mlstm_bwd" if reverse else "mlstm_fwd",
    )(*args)


def _row_scan_sum(x, reverse):
    n = x.shape[0]
    row = lax.broadcasted_iota(jnp.int32, x.shape, 0)
    sh = 1
    while sh < n:
        if reverse:
            x = x + jnp.where(row < n - sh, pltpu.roll(x, n - sh, axis=0), 0.0)
        else:
            x = x + jnp.where(row >= sh, pltpu.roll(x, sh, axis=0), 0.0)
        sh *= 2
    return x


def _gla_kernel(q_ref, k_ref, v_ref, zs_ref, wlr_ref, blr_ref, *rest, reverse):
    if reverse:
        hf_ref, gg_ref, gain_ref, out_ref, st_scr, hb_scr = rest
    else:
        out_ref, st_scr = rest
        hb_scr = out_ref
    C = G_CHUNK
    dk = G_KEY_DIM
    dv = G_VAL_DIM
    nsub = ROW_BLOCK // C

    @pl.when(pl.program_id(1) == 0)
    def _():
        st_scr[...] = jnp.zeros(st_scr.shape, F32)

    row = lax.broadcasted_iota(jnp.int32, (C, dk), 0)
    lane_c = lax.broadcasted_iota(jnp.int32, (C, C), 1)

    def body(t, carry):
        c = (nsub - 1 - t) if reverse else t
        rs = pl.ds(pl.multiple_of(c * C, C), C)
        pre = jnp.dot(zs_ref[rs, :].astype(BF16), wlr_ref[...],
                      preferred_element_type=F32) + blr_ref[...]
        la_all = _log_sigmoid(pre) / G_GATE_NORM
        g_all = _row_scan_sum(la_all, reverse)
        for h in range(G_HEADS):
            ksl = slice(h * dk, (h + 1) * dk)
            vsl = slice(h * dv, (h + 1) * dv)
            gcum = g_all[:, ksl]
            q = q_ref[rs, ksl] * (dk ** -0.5)
            k = k_ref[rs, ksl]
            v_b = v_ref[rs, vsl].astype(BF16)
            st_old = st_scr[h]

            att = jnp.zeros((C, C), F32)
            for s in range(C):
                keep = (row <= s) if reverse else (row >= s)
                e = jnp.exp(jnp.where(keep, gcum - gcum[s:s + 1, :], NEG))
                col = jnp.sum(q * k[s:s + 1, :] * e, axis=1, keepdims=True)
                att = jnp.where(lane_c == s, col, att)

            g_last = gcum[0:1, :] if reverse else gcum[C - 1:C, :]
            qg = (q * jnp.exp(gcum)).astype(BF16)
            o = (jnp.dot(att.astype(BF16), v_b, preferred_element_type=F32)
                 + lax.dot_general(qg, st_old.astype(BF16), (((1,), (1,)), ((), ())),
                                   preferred_element_type=F32))
            kw = (k * jnp.exp(g_last - gcum)).astype(BF16)
            st_scr[h] = (jnp.exp(g_last) * st_old
                         + lax.dot_general(v_b, kw, (((0,), (0,)), ((), ())),
                                           preferred_element_type=F32))
            hb_scr[rs, vsl] = o
        return carry

    lax.fori_loop(0, nsub, body, 0)

    if reverse:
        for h in range(G_HEADS):
            vsl = slice(h * dv, (h + 1) * dv)
            gg = gg_ref[:, vsl]
            out_ref[:, vsl] = _head_norm_gate(hf_ref[:, vsl] + hb_scr[:, vsl], gain_ref[:, vsl],
                                              gg * _sigmoid(gg))


def _gla(z, zs, wlr, blr, row_off, batch, seq, *, reverse, hf=None, gain=None):
    L = ROW_BLOCK
    nblk = seq // L
    off = row_off // L
    q_blk = 4 * M_WIDTH // G_KWIDTH
    v_blk = (4 * M_WIDTH + 2 * G_KWIDTH) // G_VWIDTH

    def rows(bi, n):
        return off + bi * nblk + (nblk - 1 - n if reverse else n)

    in_specs = [
        pl.BlockSpec((L, G_KWIDTH), lambda bi, n: (rows(bi, n), q_blk)),
        pl.BlockSpec((L, G_KWIDTH), lambda bi, n: (rows(bi, n), q_blk + 1)),
        pl.BlockSpec((L, G_VWIDTH), lambda bi, n: (rows(bi, n), v_blk)),
        pl.BlockSpec((L, SMALL_WIDTH), lambda bi, n: (rows(bi, n), 0)),
        pl.BlockSpec((SMALL_WIDTH, G_KWIDTH), lambda bi, n: (0, 0)),
        pl.BlockSpec((1, G_KWIDTH), lambda bi, n: (0, 0)),
    ]
    args = [z, z, z, zs, wlr, blr]
    scratch = [pltpu.VMEM((G_HEADS, G_VAL_DIM, G_KEY_DIM), F32)]
    if reverse:
        in_specs += [
            pl.BlockSpec((L, G_VWIDTH), lambda bi, n: (rows(bi, n) - off, 0)),
            pl.BlockSpec((L, G_VWIDTH), lambda bi, n: (rows(bi, n), v_blk + 1)),
            pl.BlockSpec((1, G_VWIDTH), lambda bi, n: (0, 0)),
        ]
        args += [hf, z, gain]
        scratch.append(pltpu.VMEM((L, G_VWIDTH), F32))
    return pl.pallas_call(
        functools.partial(_gla_kernel, reverse=reverse),
        grid=(batch, nblk),
        in_specs=in_specs,
        out_specs=pl.BlockSpec((L, G_VWIDTH), lambda bi, n: (rows(bi, n) - off, 0)),
        out_shape=jax.ShapeDtypeStruct((batch * seq, G_VWIDTH), BF16 if reverse else F32),
        scratch_shapes=scratch,
        compiler_params=_cparams(("parallel", "arbitrary")),
        name="gla_bwd" if reverse else "gla_fwd",
    )(*args)


def _pack_layer(w_in, b_in, w_lr2, b_lr2):
    o_mi = 4 * M_WIDTH
    o_gq = o_mi + 4 * M_HEADS
    o_lr = o_gq + 2 * G_KWIDTH + 2 * G_VWIDTH
    o_ga = o_lr + 2 * G_LOWRANK
    n_small = 4 * M_HEADS + 2 * G_LOWRANK

    def big(t):
        return jnp.concatenate([t[..., :o_mi], t[..., o_gq:o_lr], t[..., o_ga:]], axis=-1)

    def small(t):
        s = jnp.concatenate([t[..., o_mi:o_gq], t[..., o_lr:o_ga]], axis=-1)
        pad = [(0, 0)] * (t.ndim - 1) + [(0, SMALL_WIDTH - n_small)]
        return jnp.pad(s, pad)

    w_big = big(w_in).astype(BF16)
    b_big = big(b_in)[None, :]
    w_small = small(w_in).astype(BF16)
    b_small = small(b_in)
    wlr = []
    for d in range(2):
        lo = 4 * M_HEADS + d * G_LOWRANK
        wlr.append(jnp.zeros((SMALL_WIDTH, G_KWIDTH), F32).at[lo:lo + G_LOWRANK].set(w_lr2[d])
                   .astype(BF16))
    return (w_big, b_big, w_small, b_small[None, :], w_small.T, b_small[:, None],
            wlr, [b_lr2[0][None, :], b_lr2[1][None, :]])


def kernel(x_prompt, x_sample, ln1, w_in, b_in, m_norm, w_lr2, b_lr2, g_norm, w_pa, w_pb, w_o,
           ln2, w_gu, w_down, ln_f):
    bp, sp, _ = x_prompt.shape
    bs, ss, _ = x_sample.shape
    tp = bp * sp
    x = jnp.concatenate([x_prompt.reshape(tp, D_MODEL), x_sample.reshape(bs * ss, D_MODEL)], axis=0)
    groups = ((0, bp, sp), (tp, bs, ss))

    for l in range(ln1.shape[0]):
        (w_big, b_big, w_small, b_small, w_small_t, b_small_t, wlr, blr) = _pack_layer(
            w_in[l], b_in[l], w_lr2[l], b_lr2[l])
        z, zs, zst = _inproj(x, ln1[l][None, :], w_big, b_big, w_small, b_small,
                             w_small_t, b_small_t)

        m_gain = m_norm[l][None, :]
        g_gain = g_norm[l][None, :]
        ha_parts, hb_parts = [], []
        for (off, nb, seq) in groups:
            hf = _mlstm(z, zst, off, nb, seq, reverse=False)
            ha_parts.append(_mlstm(z, zst, off, nb, seq, reverse=True, hf=hf, gain=m_gain))
            gf = _gla(z, zs, wlr[0], blr[0], off, nb, seq, reverse=False)
            hb_parts.append(_gla(z, zs, wlr[1], blr[1], off, nb, seq, reverse=True, hf=gf,
                                 gain=g_gain))
        ha = jnp.concatenate(ha_parts, axis=0)
        hb = jnp.concatenate(hb_parts, axis=0)

        merged = _merge(ha, hb, w_pa[l].astype(BF16), w_pb[l].astype(BF16), z)
        x = _resmm(x, merged, w_o[l].astype(BF16), name="out_proj")
        hmid = _ffn_up(x, ln2[l][None, :], w_gu[l].astype(BF16))
        x = _resmm(x, hmid, w_down[l].astype(BF16), name="ffn_down")

    y = _final_norm(x, ln_f[None, :])
    return (y[:tp].reshape(bp, sp, D_MODEL), y[tp:].reshape(bs, ss, D_MODEL))
```

```python
import functools

import jax
import jax.numpy as jnp
from jax import lax
from jax.experimental import pallas as pl
from jax.experimental.pallas import tpu as pltpu

F32 = jnp.float32
BF16 = jnp.bfloat16

D_MODEL = 2048
M_HEADS = 4
M_HEAD_DIM = 256
M_WIDTH = 1024
G_HEADS = 4
G_VAL_DIM = 256
G_KEY_DIM = 128
G_VWIDTH = 1024
G_KWIDTH = 512
G_LOWRANK = 16
G_GATE_NORM = 16.0
D_FF = 5632
EPS = 1e-6
NEG = -1e30

BIG_WIDTH = 4 * M_WIDTH + 2 * G_KWIDTH + 2 * G_VWIDTH + 2 * D_MODEL
SMALL_WIDTH = 128
CHUNK = 128
SUBLANES = 8

VMEM_LIMIT = 56 * 1024 * 1024


def _cparams(sem):
    return pltpu.CompilerParams(dimension_semantics=sem, vmem_limit_bytes=VMEM_LIMIT)


def _log_sigmoid(x):
    return jnp.minimum(x, 0.0) - jnp.log1p(jnp.exp(-jnp.abs(x)))


def _sigmoid(x):
    return 1.0 / (1.0 + jnp.exp(-x))


def _rmsnorm(x, g):
    return x * lax.rsqrt(jnp.mean(x * x, axis=-1, keepdims=True) + EPS) * g


def _inproj_kernel(x_ref, g_ref, w_ref, b_ref, ws_ref, bs_ref, wst_ref, bst_ref,
                   z_ref, zs_ref, zst_ref, xn_ref):
    @pl.when(pl.program_id(1) == 0)
    def _():
        xn = _rmsnorm(x_ref[...], g_ref[...]).astype(BF16)
        xn_ref[...] = xn
        zs_ref[...] = jnp.dot(xn, ws_ref[...], preferred_element_type=F32) + bs_ref[...]
        zst_ref[...] = lax.dot_general(wst_ref[...], xn, (((1,), (1,)), ((), ())),
                                       preferred_element_type=F32) + bst_ref[...]

    z_ref[...] = jnp.dot(xn_ref[...], w_ref[...], preferred_element_type=F32) + b_ref[...]


def _inproj(x, g, w, b, ws, bs, wst, bst, *, tm=1024, tn=1024):
    T = x.shape[0]
    N = w.shape[1]
    return pl.pallas_call(
        _inproj_kernel,
        grid=(T // tm, N // tn),
        in_specs=[
            pl.BlockSpec((tm, D_MODEL), lambda i, j: (i, 0), pipeline_mode=pl.Buffered(1)),
            pl.BlockSpec((1, D_MODEL), lambda i, j: (0, 0)),
            pl.BlockSpec((D_MODEL, tn), lambda i, j: (0, j)),
            pl.BlockSpec((1, tn), lambda i, j: (0, j)),
            pl.BlockSpec((D_MODEL, SMALL_WIDTH), lambda i, j: (0, 0)),
            pl.BlockSpec((1, SMALL_WIDTH), lambda i, j: (0, 0)),
            pl.BlockSpec((SMALL_WIDTH, D_MODEL), lambda i, j: (0, 0)),
            pl.BlockSpec((SMALL_WIDTH, 1), lambda i, j: (0, 0)),
        ],
        out_specs=[
            pl.BlockSpec((tm, tn), lambda i, j: (i, j)),
            pl.BlockSpec((tm, SMALL_WIDTH), lambda i, j: (i, 0)),
            pl.BlockSpec((SMALL_WIDTH, tm), lambda i, j: (0, i)),
        ],
        out_shape=[
            jax.ShapeDtypeStruct((T, N), F32),
            jax.ShapeDtypeStruct((T, SMALL_WIDTH), F32),
            jax.ShapeDtypeStruct((SMALL_WIDTH, T), F32),
        ],
        scratch_shapes=[pltpu.VMEM((tm, D_MODEL), BF16)],
        compiler_params=_cparams(("parallel", "arbitrary")),
        name="inproj",
    )(x, g, w, b, ws, bs, wst, bst)


def _head_norm_gate(hsum, gain, gate):
    y = hsum * lax.rsqrt(jnp.mean(hsum * hsum, axis=-1, keepdims=True) + EPS)
    return (y * gain * gate).astype(BF16)


def _merge_kernel(af_ref, ab_ref, bf_ref, bb_ref, mo_ref, gg_ref, mgain_ref, ggain_ref,
                  wpa_ref, wpb_ref, ga_ref, gb_ref, out_ref, ha_scr, hb_scr):
    @pl.when(pl.program_id(1) == 0)
    def _():
        for h in range(M_HEADS):
            sl = slice(h * M_HEAD_DIM, (h + 1) * M_HEAD_DIM)
            ha_scr[:, sl] = _head_norm_gate(af_ref[:, sl] + ab_ref[:, sl], mgain_ref[:, sl],
                                            _sigmoid(mo_ref[:, sl]))
        for h in range(G_HEADS):
            sl = slice(h * G_VAL_DIM, (h + 1) * G_VAL_DIM)
            gg = gg_ref[:, sl]
            hb_scr[:, sl] = _head_norm_gate(bf_ref[:, sl] + bb_ref[:, sl], ggain_ref[:, sl],
                                            gg * _sigmoid(gg))

    pa = jnp.dot(ha_scr[...], wpa_ref[...], preferred_element_type=F32)
    pb = jnp.dot(hb_scr[...], wpb_ref[...], preferred_element_type=F32)
    out_ref[...] = (_sigmoid(ga_ref[...]) * pa + _sigmoid(gb_ref[...]) * pb).astype(BF16)


def _merge(af, ab, bf, bb, z, mgain, ggain, wpa, wpb, *, tm=512, tn=1024):
    T = af.shape[0]
    mo_blk = 3 * M_WIDTH // M_WIDTH
    gg_blk = (4 * M_WIDTH + 2 * G_KWIDTH + G_VWIDTH) // G_VWIDTH
    ga_blk = (BIG_WIDTH - 2 * D_MODEL) // tn
    gb_blk = (BIG_WIDTH - D_MODEL) // tn
    once = dict(pipeline_mode=pl.Buffered(1))
    return pl.pallas_call(
        _merge_kernel,
        grid=(T // tm, D_MODEL // tn),
        in_specs=[
            pl.BlockSpec((tm, M_WIDTH), lambda i, j: (i, 0), **once),
            pl.BlockSpec((tm, M_WIDTH), lambda i, j: (i, 0), **once),
            pl.BlockSpec((tm, G_VWIDTH), lambda i, j: (i, 0), **once),
            pl.BlockSpec((tm, G_VWIDTH), lambda i, j: (i, 0), **once),
            pl.BlockSpec((tm, M_WIDTH), lambda i, j: (i, mo_blk), **once),
            pl.BlockSpec((tm, G_VWIDTH), lambda i, j: (i, gg_blk), **once),
            pl.BlockSpec((1, M_WIDTH), lambda i, j: (0, 0)),
            pl.BlockSpec((1, G_VWIDTH), lambda i, j: (0, 0)),
            pl.BlockSpec((M_WIDTH, tn), lambda i, j: (0, j)),
            pl.BlockSpec((G_VWIDTH, tn), lambda i, j: (0, j)),
            pl.BlockSpec((tm, tn), lambda i, j: (i, ga_blk + j)),
            pl.BlockSpec((tm, tn), lambda i, j: (i, gb_blk + j)),
        ],
        out_specs=pl.BlockSpec((tm, tn), lambda i, j: (i, j)),
        out_shape=jax.ShapeDtypeStruct((T, D_MODEL), BF16),
        scratch_shapes=[pltpu.VMEM((tm, M_WIDTH), BF16), pltpu.VMEM((tm, G_VWIDTH), BF16)],
        compiler_params=_cparams(("parallel", "arbitrary")),
        name="merge",
    )(af, ab, bf, bb, z, z, mgain, ggain, wpa, wpb, z, z)


def _resmm_kernel(x_ref, a_ref, w_ref, out_ref):
    out_ref[...] = x_ref[...] + jnp.dot(a_ref[...], w_ref[...], preferred_element_type=F32)


def _resmm(x, a, w, *, tm=1024, tn=512, name="resmm"):
    T, K = a.shape
    N = w.shape[1]
    return pl.pallas_call(
        _resmm_kernel,
        grid=(T // tm, N // tn),
        in_specs=[
            pl.BlockSpec((tm, tn), lambda i, j: (i, j)),
            pl.BlockSpec((tm, K), lambda i, j: (i, 0)),
            pl.BlockSpec((K, tn), lambda i, j: (0, j)),
        ],
        out_specs=pl.BlockSpec((tm, tn), lambda i, j: (i, j)),
        out_shape=jax.ShapeDtypeStruct((T, N), F32),
        compiler_params=_cparams(("parallel", "arbitrary")),
        name=name,
    )(x, a, w)


def _ffn_up_kernel(x_ref, g_ref, wg_ref, wu_ref, h_ref, xn_ref):
    @pl.when(pl.program_id(1) == 0)
    def _():
        xn_ref[...] = _rmsnorm(x_ref[...], g_ref[...]).astype(BF16)

    xn = xn_ref[...]
    gate = jnp.dot(xn, wg_ref[...], preferred_element_type=F32)
    up = jnp.dot(xn, wu_ref[...], preferred_element_type=F32)
    h_ref[...] = (gate * _sigmoid(gate) * up).astype(BF16)


def _ffn_up(x, g, wgu, *, tm=1024, tn=512):
    T = x.shape[0]
    nj = D_FF // tn
    return pl.pallas_call(
        _ffn_up_kernel,
        grid=(T // tm, nj),
        in_specs=[
            pl.BlockSpec((tm, D_MODEL), lambda i, j: (i, 0), pipeline_mode=pl.Buffered(1)),
            pl.BlockSpec((1, D_MODEL), lambda i, j: (0, 0)),
            pl.BlockSpec((D_MODEL, tn), lambda i, j: (0, j)),
            pl.BlockSpec((D_MODEL, tn), lambda i, j: (0, nj + j)),
        ],
        out_specs=pl.BlockSpec((tm, tn), lambda i, j: (i, j)),
        out_shape=jax.ShapeDtypeStruct((T, D_FF), BF16),
        scratch_shapes=[pltpu.VMEM((tm, D_MODEL), BF16)],
        compiler_params=_cparams(("parallel", "arbitrary")),
        name="ffn_up",
    )(x, g, wgu, wgu)


def _final_norm_kernel(x_ref, g_ref, out_ref, *, nbatch):
    for bi in range(nbatch):
        out_ref[bi] = _rmsnorm(x_ref[bi * CHUNK:(bi + 1) * CHUNK, :], g_ref[...])


def _final_norm(x, g, blk_off, nbatch, seq):
    nblk = seq // CHUNK
    return pl.pallas_call(
        functools.partial(_final_norm_kernel, nbatch=nbatch),
        grid=(nblk,),
        in_specs=[
            pl.BlockSpec((nbatch * CHUNK, D_MODEL), lambda i: (blk_off + i, 0)),
            pl.BlockSpec((1, D_MODEL), lambda i: (0, 0)),
        ],
        out_specs=pl.BlockSpec((nbatch, CHUNK, D_MODEL), lambda i: (0, i, 0)),
        out_shape=jax.ShapeDtypeStruct((nbatch, seq, D_MODEL), F32),
        compiler_params=_cparams(("parallel",)),
        name="final_norm",
    )(x, g)


def _lane_scan(x, op, fill, reverse):
    n = x.shape[1]
    lane = lax.broadcasted_iota(jnp.int32, x.shape, 1)
    sh = 1
    while sh < n:
        if reverse:
            x = op(x, jnp.where(lane < n - sh, pltpu.roll(x, n - sh, axis=1), fill))
        else:
            x = op(x, jnp.where(lane >= sh, pltpu.roll(x, sh, axis=1), fill))
        sh *= 2
    return x


def _to_column(row, eye):
    return jnp.sum(jnp.where(eye, row, 0.0), axis=1, keepdims=True)


def _mlstm_kernel(qf_ref, kf_ref, vf_ref, gtf_ref, qb_ref, kb_ref, vb_ref, gtb_ref,
                  hf_ref, hb_ref, c_scr, n_scr, m_scr, *, nbatch, n_reset):
    L = CHUNK
    dh = M_HEAD_DIM
    step = pl.program_id(0)

    @pl.when((step == 0) | (step == n_reset))
    def _():
        c_scr[...] = jnp.zeros(c_scr.shape, F32)
        n_scr[...] = jnp.zeros(n_scr.shape, F32)
        m_scr[...] = jnp.full(m_scr.shape, NEG, F32)

    rr = lax.broadcasted_iota(jnp.int32, (L, L), 0)
    cc = lax.broadcasted_iota(jnp.int32, (L, L), 1)
    eye = rr == cc

    dirs = ((qf_ref, kf_ref, vf_ref, gtf_ref, hf_ref), (qb_ref, kb_ref, vb_ref, gtb_ref, hb_ref))
    for d, (q_ref, k_ref, v_ref, gt_ref, out_ref) in enumerate(dirs):
        reverse = d == 1
        mask = (cc >= rr) if reverse else (cc <= rr)
        for bi in range(nbatch):
            rows = slice(bi * L, (bi + 1) * L)
            i_rows = gt_ref[0:8, rows]
            lf = _log_sigmoid(gt_ref[8:16, rows])
            a = _lane_scan(lf, jnp.add, 0.0, reverse)
            a_tot = jnp.broadcast_to(jnp.sum(lf, axis=1, keepdims=True), (8, L))
            m_old = m_scr[d * nbatch + bi]
            g = a_tot - a + i_rows
            m_new = jnp.maximum(a_tot + m_old, jnp.max(g, axis=1, keepdims=True))
            b = i_rows - a
            big_m = jnp.maximum(m_old, _lane_scan(b, jnp.maximum, -jnp.inf, reverse))
            e_int = jnp.exp(m_old - big_m)
            floor = jnp.exp(-(a + big_m))
            ksc = jnp.exp(g - m_new)
            decay = jnp.exp(a_tot + m_old - m_new)
            m_scr[d * nbatch + bi] = m_new

            for h in range(M_HEADS):
                r = d * M_HEADS + h
                ci = (d * nbatch + bi) * M_HEADS + h
                sl = slice(h * dh, (h + 1) * dh)
                m_c = _to_column(big_m[r:r + 1, :], eye)
                e_c = _to_column(e_int[r:r + 1, :], eye)
                f_c = _to_column(floor[r:r + 1, :], eye)
                k_c = _to_column(ksc[r:r + 1, :], eye)
                dec = jnp.concatenate([decay[r:r + 1, :]] * (dh // L), axis=1)

                q = q_ref[rows, sl] * (dh ** -0.5)
                k = k_ref[rows, sl]
                v_b = v_ref[rows, sl].astype(BF16)
                q_b = q.astype(BF16)
                c_old = c_scr[ci]
                n_old = n_scr[ci]

                qk = lax.dot_general(q_b, k.astype(BF16), (((1,), (1,)), ((), ())),
                                     preferred_element_type=F32)
                p = jnp.exp(jnp.where(mask, b[r:r + 1, :] - m_c, NEG))
                s = qk * p
                qn = jnp.sum(q * n_old, axis=1, keepdims=True)
                den = jnp.sum(s, axis=1, keepdims=True) + e_c * qn
                num = (jnp.dot(s.astype(BF16), v_b, preferred_element_type=F32)
                       + e_c * jnp.dot(q_b, c_old.astype(BF16), preferred_element_type=F32))
                out_ref[rows, sl] = num / jnp.maximum(jnp.abs(den), f_c)

                kw = k * k_c
                c_scr[ci] = dec * c_old + lax.dot_general(
                    kw.astype(BF16), v_b, (((0,), (0,)), ((), ())), preferred_element_type=F32)
                n_scr[ci] = dec * n_old + jnp.sum(kw, axis=0, keepdims=True)


def _scan_blocks(n_first, n_total):
    def fwd(n):
        return n

    def bwd(n):
        return jnp.where(n < n_first, n_first - 1 - n, n_first + n_total - 1 - n)

    return fwd, bwd


def _mlstm(z, zst, nbatch, n_first, n_total):
    T = z.shape[0]
    R = nbatch * CHUNK
    fwd, bwd = _scan_blocks(n_first, n_total)
    in_specs = []
    for blk in (fwd, bwd):
        in_specs += [
            pl.BlockSpec((R, M_WIDTH), lambda n, blk=blk: (blk(n), 0)),
            pl.BlockSpec((R, M_WIDTH), lambda n, blk=blk: (blk(n), 1)),
            pl.BlockSpec((R, M_WIDTH), lambda n, blk=blk: (blk(n), 2)),
            pl.BlockSpec((SMALL_WIDTH, R), lambda n, blk=blk: (0, blk(n))),
        ]
    nchain = 2 * nbatch * M_HEADS
    return pl.pallas_call(
        functools.partial(_mlstm_kernel, nbatch=nbatch, n_reset=n_first),
        grid=(n_total,),
        in_specs=in_specs,
        out_specs=[
            pl.BlockSpec((R, M_WIDTH), lambda n: (fwd(n), 0)),
            pl.BlockSpec((R, M_WIDTH), lambda n: (bwd(n), 0)),
        ],
        out_shape=[jax.ShapeDtypeStruct((T, M_WIDTH), F32)] * 2,
        scratch_shapes=[
            pltpu.VMEM((nchain, M_HEAD_DIM, M_HEAD_DIM), F32),
            pltpu.VMEM((nchain, 1, M_HEAD_DIM), F32),
            pltpu.VMEM((2 * nbatch, 8, CHUNK), F32),
        ],
        compiler_params=_cparams(("arbitrary",)),
        name="mlstm",
    )(z, z, z, zst, z, z, z, zst)


def _row_scan_sum(x, reverse):
    n = x.shape[0]
    row = lax.broadcasted_iota(jnp.int32, x.shape, 0)
    sh = 1
    while sh < n:
        if reverse:
            x = x + jnp.where(row < n - sh, pltpu.roll(x, n - sh, axis=0), 0.0)
        else:
            x = x + jnp.where(row >= sh, pltpu.roll(x, sh, axis=0), 0.0)
        sh *= 2
    return x


def _gla_attention(q, k, gcum, reverse, level_masks, diag_masks):
    L, dk = q.shape
    row = lax.broadcasted_iota(jnp.int32, (L, dk), 0)
    att = jnp.zeros((L, L), F32)
    for lb, lmask in level_masks:
        half = 1 << lb
        pieces = []
        for p in range(L // (2 * half)):
            r = p * 2 * half + (half if reverse else half - 1)
            pieces.append(jnp.broadcast_to(gcum[r:r + 1, :], (2 * half, dk)))
        gref = jnp.concatenate(pieces, axis=0)
        e = jnp.exp(-jnp.abs(gcum - gref))
        q_side = ((row >> lb) & 1) == (0 if reverse else 1)
        x = (jnp.where(q_side, q, k) * e).astype(BF16)
        prod = lax.dot_general(x, x, (((1,), (1,)), ((), ())), preferred_element_type=F32)
        att = jnp.where(lmask, prod, att)

    nb = L // SUBLANES
    g3 = gcum.reshape(nb, SUBLANES, dk)
    q3 = q.reshape(nb, SUBLANES, dk)
    k3 = k.reshape(nb, SUBLANES, dk)
    att3 = att.reshape(nb, SUBLANES, L)
    for s in range(SUBLANES):
        e = jnp.exp(-jnp.abs(g3 - g3[:, s:s + 1, :]))
        col = jnp.sum(q3 * k3[:, s:s + 1, :] * e, axis=-1, keepdims=True)
        att3 = jnp.where(diag_masks[s], col, att3)
    return att3.reshape(L, L)


def _gla_masks(L, reverse):
    rr = lax.broadcasted_iota(jnp.int32, (L, L), 0)
    cc = lax.broadcasted_iota(jnp.int32, (L, L), 1)
    x = rr ^ cc
    causal = (rr < cc) if reverse else (rr > cc)
    levels = []
    lb = L.bit_length() - 2
    while (1 << lb) >= SUBLANES:
        levels.append((lb, causal & ((x >> lb) == 1)))
        lb -= 1
    nb = L // SUBLANES
    blk = lax.broadcasted_iota(jnp.int32, (nb, SUBLANES, L), 0)
    sub = lax.broadcasted_iota(jnp.int32, (nb, SUBLANES, L), 1)
    rel = lax.broadcasted_iota(jnp.int32, (nb, SUBLANES, L), 2) - blk * SUBLANES
    diag = [(rel == s) & ((sub <= s) if reverse else (sub >= s)) for s in range(SUBLANES)]
    return levels, diag


def _gla_kernel(qf_ref, kf_ref, vf_ref, zsf_ref, qb_ref, kb_ref, vb_ref, zsb_ref,
                wlrf_ref, wlrb_ref, blrf_ref, blrb_ref, of_ref, ob_ref, st_scr,
                *, nbatch, n_reset):
    L = CHUNK
    dk = G_KEY_DIM
    dv = G_VAL_DIM
    step = pl.program_id(0)

    @pl.when((step == 0) | (step == n_reset))
    def _():
        st_scr[...] = jnp.zeros(st_scr.shape, F32)

    dirs = ((qf_ref, kf_ref, vf_ref, zsf_ref, wlrf_ref, blrf_ref, of_ref),
            (qb_ref, kb_ref, vb_ref, zsb_ref, wlrb_ref, blrb_ref, ob_ref))
    for d, (q_ref, k_ref, v_ref, zs_ref, wlr_ref, blr_ref, out_ref) in enumerate(dirs):
        reverse = d == 1
        level_masks, diag_masks = _gla_masks(L, reverse)
        for bi in range(nbatch):
            rows = slice(bi * L, (bi + 1) * L)
            pre = jnp.dot(zs_ref[rows, :].astype(BF16), wlr_ref[...],
                          preferred_element_type=F32) + blr_ref[...]
            g_all = _row_scan_sum(_log_sigmoid(pre) / G_GATE_NORM, reverse)
            for h in range(G_HEADS):
                ci = (d * nbatch + bi) * G_HEADS + h
                ksl = slice(h * dk, (h + 1) * dk)
                vsl = slice(h * dv, (h + 1) * dv)
                gcum = g_all[:, ksl]
                q = q_ref[rows, ksl] * (dk ** -0.5)
                k = k_ref[rows, ksl]
                v_b = v_ref[rows, vsl].astype(BF16)
                st_old = st_scr[ci]

                att = _gla_attention(q, k, gcum, reverse, level_masks, diag_masks)
                g_last = gcum[0:1, :] if reverse else gcum[L - 1:L, :]
                qg = (q * jnp.exp(gcum)).astype(BF16)
                out_ref[rows, vsl] = (
                    jnp.dot(att.astype(BF16), v_b, preferred_element_type=F32)
                    + lax.dot_general(qg, st_old.astype(BF16), (((1,), (1,)), ((), ())),
                                      preferred_element_type=F32))
                kw = (k * jnp.exp(g_last - gcum)).astype(BF16)
                st_scr[ci] = (jnp.exp(g_last) * st_old
                              + lax.dot_general(v_b, kw, (((0,), (0,)), ((), ())),
                                                preferred_element_type=F32))


def _gla(z, zs, wlr, blr, nbatch, n_first, n_total):
    T = z.shape[0]
    R = nbatch * CHUNK
    fwd, bwd = _scan_blocks(n_first, n_total)
    q_blk = 4 * M_WIDTH // G_KWIDTH
    v_blk = (4 * M_WIDTH + 2 * G_KWIDTH) // G_VWIDTH
    in_specs = []
    for blk in (fwd, bwd):
        in_specs += [
            pl.BlockSpec((R, G_KWIDTH), lambda n, blk=blk: (blk(n), q_blk)),
            pl.BlockSpec((R, G_KWIDTH), lambda n, blk=blk: (blk(n), q_blk + 1)),
            pl.BlockSpec((R, G_VWIDTH), lambda n, blk=blk: (blk(n), v_blk)),
            pl.BlockSpec((R, SMALL_WIDTH), lambda n, blk=blk: (blk(n), 0)),
        ]
    in_specs += [pl.BlockSpec((SMALL_WIDTH, G_KWIDTH), lambda n: (0, 0))] * 2
    in_specs += [pl.BlockSpec((1, G_KWIDTH), lambda n: (0, 0))] * 2
    return pl.pallas_call(
        functools.partial(_gla_kernel, nbatch=nbatch, n_reset=n_first),
        grid=(n_total,),
        in_specs=in_specs,
        out_specs=[
            pl.BlockSpec((R, G_VWIDTH), lambda n: (fwd(n), 0)),
            pl.BlockSpec((R, G_VWIDTH), lambda n: (bwd(n), 0)),
        ],
        out_shape=[jax.ShapeDtypeStruct((T, G_VWIDTH), F32)] * 2,
        scratch_shapes=[pltpu.VMEM((2 * nbatch * G_HEADS, G_VAL_DIM, G_KEY_DIM), F32)],
        compiler_params=_cparams(("arbitrary",)),
        name="gla",
    )(z, z, z, zs, z, z, z, zs, wlr[0], wlr[1], blr[0], blr[1])


def _pack_layer(w_in, b_in, w_lr2, b_lr2):
    o_mi = 4 * M_WIDTH
    o_gq = o_mi + 4 * M_HEADS
    o_lr = o_gq + 2 * G_KWIDTH + 2 * G_VWIDTH
    o_ga = o_lr + 2 * G_LOWRANK
    n_small = 4 * M_HEADS + 2 * G_LOWRANK

    def big(t):
        return jnp.concatenate([t[..., :o_mi], t[..., o_gq:o_lr], t[..., o_ga:]], axis=-1)

    def small(t):
        s = jnp.concatenate([t[..., o_mi:o_gq], t[..., o_lr:o_ga]], axis=-1)
        pad = [(0, 0)] * (t.ndim - 1) + [(0, SMALL_WIDTH - n_small)]
        return jnp.pad(s, pad)

    w_big = big(w_in).astype(BF16)
    b_big = big(b_in)[None, :]
    w_small = small(w_in).astype(BF16)
    b_small = small(b_in)
    wlr = []
    for d in range(2):
        lo = 4 * M_HEADS + d * G_LOWRANK
        wlr.append(jnp.zeros((SMALL_WIDTH, G_KWIDTH), F32).at[lo:lo + G_LOWRANK].set(w_lr2[d])
                   .astype(BF16))
    return (w_big, b_big, w_small, b_small[None, :], w_small.T, b_small[:, None],
            wlr, [b_lr2[0][None, :], b_lr2[1][None, :]])


def _interleave(x):
    B, S, D = x.shape
    return x.reshape(B, S // CHUNK, CHUNK, D).transpose(1, 0, 2, 3).reshape(B * S, D)


def kernel(x_prompt, x_sample, ln1, w_in, b_in, m_norm, w_lr2, b_lr2, g_norm, w_pa, w_pb, w_o,
           ln2, w_gu, w_down, ln_f):
    bp, sp, _ = x_prompt.shape
    bs, ss, _ = x_sample.shape
    assert bp == bs, "both request groups must have the same batch size"
    nb = bs
    n_first = ss // CHUNK
    n_total = n_first + sp // CHUNK
    x = jnp.concatenate([_interleave(x_sample), _interleave(x_prompt)], axis=0)

    for l in range(ln1.shape[0]):
        (w_big, b_big, w_small, b_small, w_small_t, b_small_t, wlr, blr) = _pack_layer(
            w_in[l], b_in[l], w_lr2[l], b_lr2[l])
        z, zs, zst = _inproj(x, ln1[l][None, :], w_big, b_big, w_small, b_small,
                             w_small_t, b_small_t)
        af, ab = _mlstm(z, zst, nb, n_first, n_total)
        bf, bb = _gla(z, zs, wlr, blr, nb, n_first, n_total)
        merged = _merge(af, ab, bf, bb, z, m_norm[l][None, :], g_norm[l][None, :],
                        w_pa[l].astype(BF16), w_pb[l].astype(BF16))
        x = _resmm(x, merged, w_o[l].astype(BF16), name="out_proj")
        hmid = _ffn_up(x, ln2[l][None, :], w_gu[l].astype(BF16))
        x = _resmm(x, hmid, w_down[l].astype(BF16), name="ffn_down")

    g_f = ln_f[None, :]
    return (_final_norm(x, g_f, n_first, nb, sp), _final_norm(x, g_f, 0, nb, ss))
```

```python
import functools

import jax
import jax.numpy as jnp
from jax import lax
from jax.experimental import pallas as pl
from jax.experimental.pallas import tpu as pltpu

F32 = jnp.float32
BF16 = jnp.bfloat16

D_MODEL = 2048
M_HEADS = 4
M_HEAD_DIM = 256
M_WIDTH = 1024
G_HEADS = 4
G_VAL_DIM = 256
G_KEY_DIM = 128
G_VWIDTH = 1024
G_KWIDTH = 512
G_LOWRANK = 16
G_GATE_NORM = 16.0
D_FF = 5632
EPS = 1e-6
NEG = -1e30

_GROUPS = (("ga", D_MODEL), ("gb", D_MODEL), ("mq", M_WIDTH), ("mk", M_WIDTH), ("mv", M_WIDTH),
           ("mo", M_WIDTH), ("gq", G_KWIDTH), ("gk", G_KWIDTH), ("gv", G_VWIDTH), ("gg", G_VWIDTH))
COL_BLOCK = {}
BIG_WIDTH = 0
for _name, _w in _GROUPS:
    assert BIG_WIDTH % _w == 0
    COL_BLOCK[_name] = BIG_WIDTH // _w
    BIG_WIDTH += _w
SMALL_WIDTH = 128
GATE_ROWS = 4 * M_HEADS + 2 * G_LOWRANK
CHUNK = 128
SUBLANES = 8

VMEM_LIMIT = 56 * 1024 * 1024


def _cparams(sem):
    return pltpu.CompilerParams(dimension_semantics=sem, vmem_limit_bytes=VMEM_LIMIT)


def _log_sigmoid(x):
    return jnp.minimum(x, 0.0) - jnp.log1p(jnp.exp(-jnp.abs(x)))


def _sigmoid(x):
    return 1.0 / (1.0 + jnp.exp(-x))


def _rmsnorm(x, g):
    return x * lax.rsqrt(jnp.mean(x * x, axis=-1, keepdims=True) + EPS) * g


def _inproj_kernel(x_ref, g_ref, w_ref, b_ref, ws_ref, bs_ref, wst_ref, bst_ref,
                   z_ref, zs_ref, zst_ref, xn_ref):
    @pl.when(pl.program_id(1) == 0)
    def _():
        xn = _rmsnorm(x_ref[...], g_ref[...]).astype(BF16)
        xn_ref[...] = xn
        zs_ref[...] = jnp.dot(xn, ws_ref[...], preferred_element_type=F32) + bs_ref[...]
        zst_ref[...] = lax.dot_general(wst_ref[...], xn, (((1,), (1,)), ((), ())),
                                       preferred_element_type=F32) + bst_ref[...]
        fwd_row = lax.broadcasted_iota(jnp.int32, (SUBLANES, CHUNK), 0) < M_HEADS
        for c in range(zst_ref.shape[1] // CHUNK):
            cols = slice(c * CHUNK, (c + 1) * CHUNK)
            i_pre = zst_ref[0:8, cols]
            lf = _log_sigmoid(zst_ref[8:16, cols])
            pre = _lane_scan(lf, jnp.add, 0.0, False)
            suf = _lane_scan(lf, jnp.add, 0.0, True)
            a = jnp.where(fwd_row, pre, suf)
            a_tot = jnp.broadcast_to(jnp.sum(lf, axis=1, keepdims=True), lf.shape)
            b = i_pre - a
            g = a_tot - a + i_pre
            cm = jnp.where(fwd_row, _lane_scan(b, jnp.maximum, -jnp.inf, False),
                           _lane_scan(b, jnp.maximum, -jnp.inf, True))
            g_max = jnp.broadcast_to(jnp.max(g, axis=1, keepdims=True), g.shape)
            for slot, val in enumerate((a, b, cm, g, a_tot, g_max)):
                lo = GATE_ROWS + slot * SUBLANES
                zst_ref[lo:lo + SUBLANES, cols] = val

    z_ref[...] = jnp.dot(xn_ref[...], w_ref[...], preferred_element_type=F32) + b_ref[...]


def _inproj(x, g, w, b, ws, bs, wst, bst, *, tm=1024, tn=1024):
    T = x.shape[0]
    N = w.shape[1]
    return pl.pallas_call(
        _inproj_kernel,
        grid=(T // tm, N // tn),
        in_specs=[
            pl.BlockSpec((tm, D_MODEL), lambda i, j: (i, 0), pipeline_mode=pl.Buffered(1)),
            pl.BlockSpec((1, D_MODEL), lambda i, j: (0, 0)),
            pl.BlockSpec((D_MODEL, tn), lambda i, j: (0, j)),
            pl.BlockSpec((1, tn), lambda i, j: (0, j)),
            pl.BlockSpec((D_MODEL, SMALL_WIDTH), lambda i, j: (0, 0)),
            pl.BlockSpec((1, SMALL_WIDTH), lambda i, j: (0, 0)),
            pl.BlockSpec((SMALL_WIDTH, D_MODEL), lambda i, j: (0, 0)),
            pl.BlockSpec((SMALL_WIDTH, 1), lambda i, j: (0, 0)),
        ],
        out_specs=[
            pl.BlockSpec((tm, tn), lambda i, j: (i, j)),
            pl.BlockSpec((tm, SMALL_WIDTH), lambda i, j: (i, 0)),
            pl.BlockSpec((SMALL_WIDTH, tm), lambda i, j: (0, i)),
        ],
        out_shape=[
            jax.ShapeDtypeStruct((T, N), F32),
            jax.ShapeDtypeStruct((T, SMALL_WIDTH), F32),
            jax.ShapeDtypeStruct((SMALL_WIDTH, T), F32),
        ],
        scratch_shapes=[pltpu.VMEM((tm, D_MODEL), BF16)],
        compiler_params=_cparams(("parallel", "arbitrary")),
        name="inproj",
    )(x, g, w, b, ws, bs, wst, bst)


def _head_norm_gate(hsum, gain, gate):
    y = hsum * lax.rsqrt(jnp.mean(hsum * hsum, axis=-1, keepdims=True) + EPS)
    return (y * gain * gate).astype(BF16)


def _mixout_kernel(x_ref, af_ref, ab_ref, bf_ref, bb_ref, mo_ref, gg_ref, ga_ref, gb_ref,
                   mgain_ref, ggain_ref, wpa_ref, wpb_ref, wo_ref, out_ref, ha_scr, hb_scr):
    for h in range(M_HEADS):
        sl = slice(h * M_HEAD_DIM, (h + 1) * M_HEAD_DIM)
        ha_scr[:, sl] = _head_norm_gate(af_ref[:, sl] + ab_ref[:, sl], mgain_ref[:, sl],
                                        _sigmoid(mo_ref[:, sl]))
    for h in range(G_HEADS):
        sl = slice(h * G_VAL_DIM, (h + 1) * G_VAL_DIM)
        gg = gg_ref[:, sl]
        hb_scr[:, sl] = _head_norm_gate(bf_ref[:, sl] + bb_ref[:, sl], ggain_ref[:, sl],
                                        gg * _sigmoid(gg))
    pa = jnp.dot(ha_scr[...], wpa_ref[...], preferred_element_type=F32)
    pb = jnp.dot(hb_scr[...], wpb_ref[...], preferred_element_type=F32)
    merged = (_sigmoid(ga_ref[...]) * pa + _sigmoid(gb_ref[...]) * pb).astype(BF16)
    out_ref[...] = x_ref[...] + jnp.dot(merged, wo_ref[...], preferred_element_type=F32)


def _mixout(x, af, ab, bf, bb, z, mgain, ggain, wpa, wpb, wo, *, tm=256):
    T = x.shape[0]
    mo_blk, gg_blk, ga_blk, gb_blk = (COL_BLOCK[n] for n in ("mo", "gg", "ga", "gb"))
    once = dict(pipeline_mode=pl.Buffered(1))
    return pl.pallas_call(
        _mixout_kernel,
        grid=(T // tm,),
        in_specs=[
            pl.BlockSpec((tm, D_MODEL), lambda i: (i, 0)),
            pl.BlockSpec((tm, M_WIDTH), lambda i: (i, 0)),
            pl.BlockSpec((tm, M_WIDTH), lambda i: (i, 0)),
            pl.BlockSpec((tm, G_VWIDTH), lambda i: (i, 0)),
            pl.BlockSpec((tm, G_VWIDTH), lambda i: (i, 0)),
            pl.BlockSpec((tm, M_WIDTH), lambda i: (i, mo_blk)),
            pl.BlockSpec((tm, G_VWIDTH), lambda i: (i, gg_blk)),
            pl.BlockSpec((tm, D_MODEL), lambda i: (i, ga_blk)),
            pl.BlockSpec((tm, D_MODEL), lambda i: (i, gb_blk)),
            pl.BlockSpec((1, M_WIDTH), lambda i: (0, 0)),
            pl.BlockSpec((1, G_VWIDTH), lambda i: (0, 0)),
            pl.BlockSpec((M_WIDTH, D_MODEL), lambda i: (0, 0), **once),
            pl.BlockSpec((G_VWIDTH, D_MODEL), lambda i: (0, 0), **once),
            pl.BlockSpec((D_MODEL, D_MODEL), lambda i: (0, 0), **once),
        ],
        out_specs=pl.BlockSpec((tm, D_MODEL), lambda i: (i, 0)),
        out_shape=jax.ShapeDtypeStruct((T, D_MODEL), F32),
        scratch_shapes=[pltpu.VMEM((tm, M_WIDTH), BF16), pltpu.VMEM((tm, G_VWIDTH), BF16)],
        compiler_params=_cparams(("parallel",)),
        name="mixout",
    )(x, af, ab, bf, bb, z, z, z, z, mgain, ggain, wpa, wpb, wo)


def _resmm_kernel(x_ref, a_ref, w_ref, out_ref):
    out_ref[...] = x_ref[...] + jnp.dot(a_ref[...], w_ref[...], preferred_element_type=F32)


def _resmm(x, a, w, *, tm=1024, tn=512, name="resmm"):
    T, K = a.shape
    N = w.shape[1]
    return pl.pallas_call(
        _resmm_kernel,
        grid=(T // tm, N // tn),
        in_specs=[
            pl.BlockSpec((tm, tn), lambda i, j: (i, j)),
            pl.BlockSpec((tm, K), lambda i, j: (i, 0)),
            pl.BlockSpec((K, tn), lambda i, j: (0, j)),
        ],
        out_specs=pl.BlockSpec((tm, tn), lambda i, j: (i, j)),
        out_shape=jax.ShapeDtypeStruct((T, N), F32),
        compiler_params=_cparams(("parallel", "arbitrary")),
        name=name,
    )(x, a, w)


def _ffn_up_kernel(x_ref, g_ref, wg_ref, wu_ref, h_ref, xn_ref):
    @pl.when(pl.program_id(1) == 0)
    def _():
        xn_ref[...] = _rmsnorm(x_ref[...], g_ref[...]).astype(BF16)

    xn = xn_ref[...]
    gate = jnp.dot(xn, wg_ref[...], preferred_element_type=F32)
    up = jnp.dot(xn, wu_ref[...], preferred_element_type=F32)
    h_ref[...] = (gate * _sigmoid(gate) * up).astype(BF16)


def _ffn_up(x, g, wgu, *, tm=1024, tn=512):
    T = x.shape[0]
    nj = D_FF // tn
    return pl.pallas_call(
        _ffn_up_kernel,
        grid=(T // tm, nj),
        in_specs=[
            pl.BlockSpec((tm, D_MODEL), lambda i, j: (i, 0), pipeline_mode=pl.Buffered(1)),
            pl.BlockSpec((1, D_MODEL), lambda i, j: (0, 0)),
            pl.BlockSpec((D_MODEL, tn), lambda i, j: (0, j)),
            pl.BlockSpec((D_MODEL, tn), lambda i, j: (0, nj + j)),
        ],
        out_specs=pl.BlockSpec((tm, tn), lambda i, j: (i, j)),
        out_shape=jax.ShapeDtypeStruct((T, D_FF), BF16),
        scratch_shapes=[pltpu.VMEM((tm, D_MODEL), BF16)],
        compiler_params=_cparams(("parallel", "arbitrary")),
        name="ffn_up",
    )(x, g, wgu, wgu)


def _final_norm_kernel(x_ref, g_ref, out_ref, *, nbatch):
    for bi in range(nbatch):
        out_ref[bi] = _rmsnorm(x_ref[bi * CHUNK:(bi + 1) * CHUNK, :], g_ref[...])


def _final_norm(x, g, blk_off, nbatch, seq):
    nblk = seq // CHUNK
    return pl.pallas_call(
        functools.partial(_final_norm_kernel, nbatch=nbatch),
        grid=(nblk,),
        in_specs=[
            pl.BlockSpec((nbatch * CHUNK, D_MODEL), lambda i: (blk_off + i, 0)),
            pl.BlockSpec((1, D_MODEL), lambda i: (0, 0)),
        ],
        out_specs=pl.BlockSpec((nbatch, CHUNK, D_MODEL), lambda i: (0, i, 0)),
        out_shape=jax.ShapeDtypeStruct((nbatch, seq, D_MODEL), F32),
        compiler_params=_cparams(("parallel",)),
        name="final_norm",
    )(x, g)


def _lane_scan(x, op, fill, reverse):
    n = x.shape[1]
    lane = lax.broadcasted_iota(jnp.int32, x.shape, 1)
    sh = 1
    while sh < n:
        if reverse:
            x = op(x, jnp.where(lane < n - sh, pltpu.roll(x, n - sh, axis=1), fill))
        else:
            x = op(x, jnp.where(lane >= sh, pltpu.roll(x, sh, axis=1), fill))
        sh *= 2
    return x


def _to_column(row, eye):
    return jnp.sum(jnp.where(eye, row, 0.0), axis=1, keepdims=True)


def _mlstm_kernel(qf_ref, kf_ref, vf_ref, gtf_ref, qb_ref, kb_ref, vb_ref, gtb_ref,
                  hf_ref, hb_ref, c_scr, n_scr, m_scr, *, nbatch, n_reset):
    L = CHUNK
    dh = M_HEAD_DIM
    step = pl.program_id(0)

    @pl.when((step == 0) | (step == n_reset))
    def _():
        c_scr[...] = jnp.zeros(c_scr.shape, F32)
        n_scr[...] = jnp.zeros(n_scr.shape, F32)
        m_scr[...] = jnp.full(m_scr.shape, NEG, F32)

    rr = lax.broadcasted_iota(jnp.int32, (L, L), 0)
    cc = lax.broadcasted_iota(jnp.int32, (L, L), 1)
    eye = rr == cc

    dirs = ((qf_ref, kf_ref, vf_ref, gtf_ref, hf_ref), (qb_ref, kb_ref, vb_ref, gtb_ref, hb_ref))
    for d, (q_ref, k_ref, v_ref, gt_ref, out_ref) in enumerate(dirs):
        reverse = d == 1
        mask = (cc >= rr) if reverse else (cc <= rr)
        for bi in range(nbatch):
            rows = slice(bi * L, (bi + 1) * L)
            a, b, cm, g, a_tot, g_max = [
                gt_ref[GATE_ROWS + t * SUBLANES:GATE_ROWS + (t + 1) * SUBLANES, rows]
                for t in range(6)]
            m_old = m_scr[d * nbatch + bi]
            m_new = jnp.maximum(a_tot + m_old, g_max)
            big_m = jnp.maximum(m_old, cm)
            e_int = jnp.exp(m_old - big_m)
            floor = jnp.exp(-(a + big_m))
            ksc = jnp.exp(g - m_new)
            decay = jnp.exp(a_tot + m_old - m_new)
            m_scr[d * nbatch + bi] = m_new

            for h in range(M_HEADS):
                r = d * M_HEADS + h
                ci = (d * nbatch + bi) * M_HEADS + h
                sl = slice(h * dh, (h + 1) * dh)
                m_c = _to_column(big_m[r:r + 1, :], eye)
                e_c = _to_column(e_int[r:r + 1, :], eye)
                f_c = _to_column(floor[r:r + 1, :], eye)
                k_c = _to_column(ksc[r:r + 1, :], eye)
                dec = jnp.concatenate([decay[r:r + 1, :]] * (dh // L), axis=1)

                q = q_ref[rows, sl] * (dh ** -0.5)
                k = k_ref[rows, sl]
                v_b = v_ref[rows, sl].astype(BF16)
                q_b = q.astype(BF16)
                c_old = c_scr[ci]
                n_old = n_scr[ci]

                qk = lax.dot_general(q_b, k.astype(BF16), (((1,), (1,)), ((), ())),
                                     preferred_element_type=F32)
                p = jnp.exp(jnp.where(mask, b[r:r + 1, :] - m_c, NEG))
                s = qk * p
                qn = jnp.sum(q * n_old, axis=1, keepdims=True)
                den = jnp.sum(s, axis=1, keepdims=True) + e_c * qn
                num = (jnp.dot(s.astype(BF16), v_b, preferred_element_type=F32)
                       + e_c * jnp.dot(q_b, c_old.astype(BF16), preferred_element_type=F32))
                out_ref[rows, sl] = num / jnp.maximum(jnp.abs(den), f_c)

                kw = k * k_c
                c_scr[ci] = dec * c_old + lax.dot_general(
                    kw.astype(BF16), v_b, (((0,), (0,)), ((), ())), preferred_element_type=F32)
                n_scr[ci] = dec * n_old + jnp.sum(kw, axis=0, keepdims=True)


def _scan_blocks(n_first, n_total):
    def fwd(n):
        return n

    def bwd(n):
        return jnp.where(n < n_first, n_first - 1 - n, n_first + n_total - 1 - n)

    return fwd, bwd


def _mlstm(z, zst, nbatch, n_first, n_total):
    T = z.shape[0]
    R = nbatch * CHUNK
    fwd, bwd = _scan_blocks(n_first, n_total)
    in_specs = []
    for blk in (fwd, bwd):
        in_specs += [
            pl.BlockSpec((R, M_WIDTH), lambda n, blk=blk: (blk(n), COL_BLOCK["mq"])),
            pl.BlockSpec((R, M_WIDTH), lambda n, blk=blk: (blk(n), COL_BLOCK["mk"])),
            pl.BlockSpec((R, M_WIDTH), lambda n, blk=blk: (blk(n), COL_BLOCK["mv"])),
            pl.BlockSpec((SMALL_WIDTH, R), lambda n, blk=blk: (0, blk(n))),
        ]
    nchain = 2 * nbatch * M_HEADS
    return pl.pallas_call(
        functools.partial(_mlstm_kernel, nbatch=nbatch, n_reset=n_first),
        grid=(n_total,),
        in_specs=in_specs,
        out_specs=[
            pl.BlockSpec((R, M_WIDTH), lambda n: (fwd(n), 0)),
            pl.BlockSpec((R, M_WIDTH), lambda n: (bwd(n), 0)),
        ],
        out_shape=[jax.ShapeDtypeStruct((T, M_WIDTH), F32)] * 2,
        scratch_shapes=[
            pltpu.VMEM((nchain, M_HEAD_DIM, M_HEAD_DIM), F32),
            pltpu.VMEM((nchain, 1, M_HEAD_DIM), F32),
            pltpu.VMEM((2 * nbatch, 8, CHUNK), F32),
        ],
        compiler_params=_cparams(("arbitrary",)),
        name="mlstm",
    )(z, z, z, zst, z, z, z, zst)


def _row_scan_sum(x, reverse):
    n = x.shape[0]
    row = lax.broadcasted_iota(jnp.int32, x.shape, 0)
    sh = 1
    while sh < n:
        if reverse:
            x = x + jnp.where(row < n - sh, pltpu.roll(x, n - sh, axis=0), 0.0)
        else:
            x = x + jnp.where(row >= sh, pltpu.roll(x, sh, axis=0), 0.0)
        sh *= 2
    return x


def _gla_attention(q, k, gcum, reverse, level_masks, eye):
    L, dk = q.shape
    nb = L // SUBLANES
    row = lax.broadcasted_iota(jnp.int32, (L, dk), 0)
    sub = lax.broadcasted_iota(jnp.int32, (nb, SUBLANES, dk), 1)
    g3 = gcum.reshape(nb, SUBLANES, dk)

    def ref_rows(half):
        first = half if reverse else half - 1
        if half >= SUBLANES:
            pieces = [jnp.broadcast_to(gcum[p * 2 * half + first:p * 2 * half + first + 1, :],
                                       (2 * half, dk)) for p in range(L // (2 * half))]
            return jnp.concatenate(pieces, axis=0)
        if half == 1:
            if reverse:
                ref3 = jnp.where((sub & 1) == 0, pltpu.roll(g3, SUBLANES - 1, axis=1), g3)
            else:
                ref3 = jnp.where((sub & 1) == 1, pltpu.roll(g3, 1, axis=1), g3)
            return ref3.reshape(L, dk)
        ref3 = None
        for p in range(SUBLANES // (2 * half)):
            r = p * 2 * half + first
            piece = jnp.broadcast_to(g3[:, r:r + 1, :], g3.shape)
            ref3 = piece if ref3 is None else jnp.where(sub >= p * 2 * half, piece, ref3)
        return ref3.reshape(L, dk)

    att = jnp.where(eye, jnp.sum(q * k, axis=1, keepdims=True), 0.0)
    for lb, lmask in level_masks:
        e = jnp.exp(-jnp.abs(gcum - ref_rows(1 << lb)))
        q_side = ((row >> lb) & 1) == (0 if reverse else 1)
        x = (jnp.where(q_side, q, k) * e).astype(BF16)
        prod = lax.dot_general(x, x, (((1,), (1,)), ((), ())), preferred_element_type=F32)
        att = jnp.where(lmask, prod, att)
    return att


def _gla_masks(L, reverse):
    rr = lax.broadcasted_iota(jnp.int32, (L, L), 0)
    cc = lax.broadcasted_iota(jnp.int32, (L, L), 1)
    x = rr ^ cc
    causal = (rr < cc) if reverse else (rr > cc)
    levels = [(lb, causal & ((x >> lb) == 1)) for lb in range(L.bit_length() - 2, -1, -1)]
    return levels, rr == cc


def _gla_kernel(qf_ref, kf_ref, vf_ref, zsf_ref, qb_ref, kb_ref, vb_ref, zsb_ref,
                wlrf_ref, wlrb_ref, blrf_ref, blrb_ref, of_ref, ob_ref, st_scr,
                *, nbatch, n_reset):
    L = CHUNK
    dk = G_KEY_DIM
    dv = G_VAL_DIM
    step = pl.program_id(0)

    @pl.when((step == 0) | (step == n_reset))
    def _():
        st_scr[...] = jnp.zeros(st_scr.shape, F32)

    dirs = ((qf_ref, kf_ref, vf_ref, zsf_ref, wlrf_ref, blrf_ref, of_ref),
            (qb_ref, kb_ref, vb_ref, zsb_ref, wlrb_ref, blrb_ref, ob_ref))
    for d, (q_ref, k_ref, v_ref, zs_ref, wlr_ref, blr_ref, out_ref) in enumerate(dirs):
        reverse = d == 1
        level_masks, eye = _gla_masks(L, reverse)
        for bi in range(nbatch):
            rows = slice(bi * L, (bi + 1) * L)
            pre = jnp.dot(zs_ref[rows, :].astype(BF16), wlr_ref[...],
                          preferred_element_type=F32) + blr_ref[...]
            g_all = _row_scan_sum(_log_sigmoid(pre) / G_GATE_NORM, reverse)
            for h in range(G_HEADS):
                ci = (d * nbatch + bi) * G_HEADS + h
                ksl = slice(h * dk, (h + 1) * dk)
                vsl = slice(h * dv, (h + 1) * dv)
                gcum = g_all[:, ksl]
                q = q_ref[rows, ksl] * (dk ** -0.5)
                k = k_ref[rows, ksl]
                v_b = v_ref[rows, vsl].astype(BF16)
                st_old = st_scr[ci]

                att = _gla_attention(q, k, gcum, reverse, level_masks, eye)
                g_last = gcum[0:1, :] if reverse else gcum[L - 1:L, :]
                qg = (q * jnp.exp(gcum)).astype(BF16)
                out_ref[rows, vsl] = (
                    jnp.dot(att.astype(BF16), v_b, preferred_element_type=F32)
                    + lax.dot_general(qg, st_old.astype(BF16), (((1,), (1,)), ((), ())),
                                      preferred_element_type=F32))
                kw = (k * jnp.exp(g_last - gcum)).astype(BF16)
                st_scr[ci] = (jnp.exp(g_last) * st_old
                              + lax.dot_general(v_b, kw, (((0,), (0,)), ((), ())),
                                                preferred_element_type=F32))


def _gla(z, zs, wlr, blr, nbatch, n_first, n_total):
    T = z.shape[0]
    R = nbatch * CHUNK
    fwd, bwd = _scan_blocks(n_first, n_total)
    in_specs = []
    for blk in (fwd, bwd):
        in_specs += [
            pl.BlockSpec((R, G_KWIDTH), lambda n, blk=blk: (blk(n), COL_BLOCK["gq"])),
            pl.BlockSpec((R, G_KWIDTH), lambda n, blk=blk: (blk(n), COL_BLOCK["gk"])),
            pl.BlockSpec((R, G_VWIDTH), lambda n, blk=blk: (blk(n), COL_BLOCK["gv"])),
            pl.BlockSpec((R, SMALL_WIDTH), lambda n, blk=blk: (blk(n), 0)),
        ]
    in_specs += [pl.BlockSpec((SMALL_WIDTH, G_KWIDTH), lambda n: (0, 0))] * 2
    in_specs += [pl.BlockSpec((1, G_KWIDTH), lambda n: (0, 0))] * 2
    return pl.pallas_call(
        functools.partial(_gla_kernel, nbatch=nbatch, n_reset=n_first),
        grid=(n_total,),
        in_specs=in_specs,
        out_specs=[
            pl.BlockSpec((R, G_VWIDTH), lambda n: (fwd(n), 0)),
            pl.BlockSpec((R, G_VWIDTH), lambda n: (bwd(n), 0)),
        ],
        out_shape=[jax.ShapeDtypeStruct((T, G_VWIDTH), F32)] * 2,
        scratch_shapes=[pltpu.VMEM((2 * nbatch * G_HEADS, G_VAL_DIM, G_KEY_DIM), F32)],
        compiler_params=_cparams(("arbitrary",)),
        name="gla",
    )(z, z, z, zs, z, z, z, zs, wlr[0], wlr[1], blr[0], blr[1])


def _pack_layer(w_in, b_in, w_lr2, b_lr2):
    o_mi = 4 * M_WIDTH
    o_gq = o_mi + 4 * M_HEADS
    o_lr = o_gq + 2 * G_KWIDTH + 2 * G_VWIDTH
    o_ga = o_lr + 2 * G_LOWRANK
    n_small = 4 * M_HEADS + 2 * G_LOWRANK

    def big(t):
        return jnp.concatenate([t[..., o_ga:], t[..., :o_mi], t[..., o_gq:o_lr]], axis=-1)

    def small(t):
        s = jnp.concatenate([t[..., o_mi:o_gq], t[..., o_lr:o_ga]], axis=-1)
        pad = [(0, 0)] * (t.ndim - 1) + [(0, SMALL_WIDTH - n_small)]
        return jnp.pad(s, pad)

    w_big = big(w_in).astype(BF16)
    b_big = big(b_in)[None, :]
    w_small = small(w_in).astype(BF16)
    b_small = small(b_in)
    wlr = []
    for d in range(2):
        lo = 4 * M_HEADS + d * G_LOWRANK
        wlr.append(jnp.zeros((SMALL_WIDTH, G_KWIDTH), F32).at[lo:lo + G_LOWRANK].set(w_lr2[d])
                   .astype(BF16))
    return (w_big, b_big, w_small, b_small[None, :], w_small.T, b_small[:, None],
            wlr, [b_lr2[0][None, :], b_lr2[1][None, :]])


def _interleave(x):
    B, S, D = x.shape
    return x.reshape(B, S // CHUNK, CHUNK, D).transpose(1, 0, 2, 3).reshape(B * S, D)


def kernel(x_prompt, x_sample, ln1, w_in, b_in, m_norm, w_lr2, b_lr2, g_norm, w_pa, w_pb, w_o,
           ln2, w_gu, w_down, ln_f):
    bp, sp, _ = x_prompt.shape
    bs, ss, _ = x_sample.shape
    assert bp == bs, "both request groups must have the same batch size"
    nb = bs
    n_first = ss // CHUNK
    n_total = n_first + sp // CHUNK
    x = jnp.concatenate([_interleave(x_sample), _interleave(x_prompt)], axis=0)

    for l in range(ln1.shape[0]):
        (w_big, b_big, w_small, b_small, w_small_t, b_small_t, wlr, blr) = _pack_layer(
            w_in[l], b_in[l], w_lr2[l], b_lr2[l])
        z, zs, zst = _inproj(x, ln1[l][None, :], w_big, b_big, w_small, b_small,
                             w_small_t, b_small_t)
        af, ab = _mlstm(z, zst, nb, n_first, n_total)
        bf, bb = _gla(z, zs, wlr, blr, nb, n_first, n_total)
        x = _mixout(x, af, ab, bf, bb, z, m_norm[l][None, :], g_norm[l][None, :],
                    w_pa[l].astype(BF16), w_pb[l].astype(BF16), w_o[l].astype(BF16))
        hmid = _ffn_up(x, ln2[l][None, :], w_gu[l].astype(BF16))
        x = _resmm(x, hmid, w_down[l].astype(BF16), name="ffn_down")

    g_f = ln_f[None, :]
    return (_final_norm(x, g_f, n_first, nb, sp), _final_norm(x, g_f, 0, nb, ss))
```

```python
import functools

import jax
import jax.numpy as jnp
from jax import lax
from jax.experimental import pallas as pl
from jax.experimental.pallas import tpu as pltpu

F32 = jnp.float32
BF16 = jnp.bfloat16

D_MODEL = 2048
M_HEADS = 4
M_HEAD_DIM = 256
M_WIDTH = 1024
G_HEADS = 4
G_VAL_DIM = 256
G_KEY_DIM = 128
G_VWIDTH = 1024
G_KWIDTH = 512
G_LOWRANK = 16
G_GATE_NORM = 16.0
D_FF = 5632
EPS = 1e-6
NEG = -1e30
LOG2E = 1.4426950408889634

_GROUPS = (("ga", D_MODEL), ("gb", D_MODEL), ("mq", M_WIDTH), ("mk", M_WIDTH), ("mv", M_WIDTH),
           ("mo", M_WIDTH), ("gq", G_KWIDTH), ("gk", G_KWIDTH), ("gv", G_VWIDTH), ("gg", G_VWIDTH))
COL_BLOCK = {}
BIG_WIDTH = 0
for _name, _w in _GROUPS:
    assert BIG_WIDTH % _w == 0
    COL_BLOCK[_name] = BIG_WIDTH // _w
    BIG_WIDTH += _w
SMALL_WIDTH = 128
GATE_ROWS = 4 * M_HEADS + 2 * G_LOWRANK
CHUNK = 128
SUBLANES = 8

VMEM_LIMIT = 56 * 1024 * 1024


def _cparams(sem):
    return pltpu.CompilerParams(dimension_semantics=sem, vmem_limit_bytes=VMEM_LIMIT)


def _log_sigmoid(x):
    return jnp.minimum(x, 0.0) - jnp.log1p(jnp.exp(-jnp.abs(x)))


def _sigmoid(x):
    return 1.0 / (1.0 + jnp.exp(-x))


def _rmsnorm(x, g):
    return x * lax.rsqrt(jnp.mean(x * x, axis=-1, keepdims=True) + EPS) * g


def _inproj_kernel(x_ref, g_ref, w_ref, b_ref, ws_ref, bs_ref, wst_ref, bst_ref, scan_ref,
                   z_ref, zs_ref, zst_ref, xn_ref):
    @pl.when(pl.program_id(1) == 0)
    def _():
        xn = _rmsnorm(x_ref[...], g_ref[...]).astype(BF16)
        xn_ref[...] = xn
        zs_ref[...] = jnp.dot(xn, ws_ref[...], preferred_element_type=F32) + bs_ref[...]
        zt = lax.dot_general(wst_ref[...], xn, (((1,), (1,)), ((), ())),
                             preferred_element_type=F32) + bst_ref[...]
        n_derived = 6 * SUBLANES
        zst_ref[0:GATE_ROWS, :] = zt[0:GATE_ROWS, :]
        zst_ref[GATE_ROWS + n_derived:, :] = zt[GATE_ROWS + n_derived:, :]
        nchunk = zst_ref.shape[1] // CHUNK
        chunks = [slice(c * CHUNK, (c + 1) * CHUNK) for c in range(nchunk)]
        i_pre = jnp.concatenate([zt[0:8, cols] for cols in chunks], axis=0)
        lf = _log_sigmoid(jnp.concatenate([zt[8:16, cols] for cols in chunks], axis=0))
        sub = lax.broadcasted_iota(jnp.int32, lf.shape, 0)
        fwd_row = (sub & (SUBLANES - 1)) < M_HEADS
        hi = lf.astype(BF16)
        r1 = lf - hi.astype(F32)
        mid = r1.astype(BF16)
        lo3 = (r1 - mid.astype(F32)).astype(BF16)
        sums = jnp.dot(jnp.concatenate([hi, mid, lo3], axis=1), scan_ref[...],
                       preferred_element_type=F32)
        a = jnp.where(fwd_row, sums[:, 0:CHUNK], sums[:, CHUNK:2 * CHUNK])
        a_tot = sums[:, 2 * CHUNK:3 * CHUNK]
        b = i_pre - a
        g = a_tot - a + i_pre
        cm = jnp.where(fwd_row, _lane_scan(b, jnp.maximum, -jnp.inf, False),
                       _lane_scan(b, jnp.maximum, -jnp.inf, True))
        g_max = jnp.broadcast_to(jnp.max(g, axis=1, keepdims=True), g.shape)
        for slot, val in enumerate((a, b, cm, g, a_tot, g_max)):
            lo = GATE_ROWS + slot * SUBLANES
            for c, cols in enumerate(chunks):
                zst_ref[lo:lo + SUBLANES, cols] = val[c * SUBLANES:(c + 1) * SUBLANES, :]

    z_ref[...] = jnp.dot(xn_ref[...], w_ref[...], preferred_element_type=F32) + b_ref[...]


def _scan_matrix():
    s = jnp.arange(CHUNK)[:, None]
    t = jnp.arange(CHUNK)[None, :]
    one = jnp.concatenate([s <= t, s >= t, jnp.ones((CHUNK, CHUNK), bool)], axis=1).astype(BF16)
    return jnp.concatenate([one, one, one], axis=0)


def _inproj(x, g, w, layer, b, ws, bs, wst, bst, scan_mat, *, tm=1024, tn=1024):
    T = x.shape[0]
    N = w.shape[2]
    return pl.pallas_call(
        _inproj_kernel,
        grid=(T // tm, N // tn),
        in_specs=[
            pl.BlockSpec((tm, D_MODEL), lambda i, j: (i, 0), pipeline_mode=pl.Buffered(1)),
            pl.BlockSpec((1, D_MODEL), lambda i, j: (0, 0)),
            pl.BlockSpec((None, D_MODEL, tn), lambda i, j: (layer, 0, j)),
            pl.BlockSpec((1, tn), lambda i, j: (0, j)),
            pl.BlockSpec((D_MODEL, SMALL_WIDTH), lambda i, j: (0, 0)),
            pl.BlockSpec((1, SMALL_WIDTH), lambda i, j: (0, 0)),
            pl.BlockSpec((SMALL_WIDTH, D_MODEL), lambda i, j: (0, 0)),
            pl.BlockSpec((SMALL_WIDTH, 1), lambda i, j: (0, 0)),
            pl.BlockSpec((3 * CHUNK, 3 * CHUNK), lambda i, j: (0, 0)),
        ],
        out_specs=[
            pl.BlockSpec((tm, tn), lambda i, j: (i, j)),
            pl.BlockSpec((tm, SMALL_WIDTH), lambda i, j: (i, 0)),
            pl.BlockSpec((SMALL_WIDTH, tm), lambda i, j: (0, i)),
        ],
        out_shape=[
            jax.ShapeDtypeStruct((T, N), F32),
            jax.ShapeDtypeStruct((T, SMALL_WIDTH), F32),
            jax.ShapeDtypeStruct((SMALL_WIDTH, T), F32),
        ],
        scratch_shapes=[pltpu.VMEM((tm, D_MODEL), BF16)],
        compiler_params=_cparams(("parallel", "arbitrary")),
        name="inproj",
    )(x, g, w, b, ws, bs, wst, bst, scan_mat)


def _head_norm_gate(hsum, gain, gate):
    y = hsum * lax.rsqrt(jnp.mean(hsum * hsum, axis=-1, keepdims=True) + EPS)
    return (y * gain * gate).astype(BF16)


def _mixout_kernel(x_ref, af_ref, ab_ref, bf_ref, bb_ref, mo_ref, gg_ref, ga_ref, gb_ref,
                   mgain_ref, ggain_ref, ln2_ref, wpa_ref, wpb_ref, wo_ref, out_ref, xn_ref,
                   ha_scr, hb_scr):
    for h in range(M_HEADS):
        sl = slice(h * M_HEAD_DIM, (h + 1) * M_HEAD_DIM)
        ha_scr[:, sl] = _head_norm_gate(af_ref[:, sl] + ab_ref[:, sl], mgain_ref[:, sl],
                                        _sigmoid(mo_ref[:, sl]))
    for h in range(G_HEADS):
        sl = slice(h * G_VAL_DIM, (h + 1) * G_VAL_DIM)
        gg = gg_ref[:, sl]
        hb_scr[:, sl] = _head_norm_gate(bf_ref[:, sl] + bb_ref[:, sl], ggain_ref[:, sl],
                                        gg * _sigmoid(gg))
    pa = jnp.dot(ha_scr[...], wpa_ref[...], preferred_element_type=F32)
    pb = jnp.dot(hb_scr[...], wpb_ref[...], preferred_element_type=F32)
    merged = (_sigmoid(ga_ref[...]) * pa + _sigmoid(gb_ref[...]) * pb).astype(BF16)
    x_new = x_ref[...] + jnp.dot(merged, wo_ref[...], preferred_element_type=F32)
    out_ref[...] = x_new
    xn_ref[...] = _rmsnorm(x_new, ln2_ref[...]).astype(BF16)


def _mixout(x, af, ab, bf, bb, z, mgain, ggain, ln2, wpa, wpb, wo, layer, *, tm=256):
    T = x.shape[0]
    mo_blk, gg_blk, ga_blk, gb_blk = (COL_BLOCK[n] for n in ("mo", "gg", "ga", "gb"))
    once = dict(pipeline_mode=pl.Buffered(1))
    return pl.pallas_call(
        _mixout_kernel,
        grid=(T // tm,),
        in_specs=[
            pl.BlockSpec((tm, D_MODEL), lambda i: (i, 0)),
            pl.BlockSpec((tm, M_WIDTH), lambda i: (i, 0)),
            pl.BlockSpec((tm, M_WIDTH), lambda i: (i, 0)),
            pl.BlockSpec((tm, G_VWIDTH), lambda i: (i, 0)),
            pl.BlockSpec((tm, G_VWIDTH), lambda i: (i, 0)),
            pl.BlockSpec((tm, M_WIDTH), lambda i: (i, mo_blk)),
            pl.BlockSpec((tm, G_VWIDTH), lambda i: (i, gg_blk)),
            pl.BlockSpec((tm, D_MODEL), lambda i: (i, ga_blk)),
            pl.BlockSpec((tm, D_MODEL), lambda i: (i, gb_blk)),
            pl.BlockSpec((1, M_WIDTH), lambda i: (0, 0)),
            pl.BlockSpec((1, G_VWIDTH), lambda i: (0, 0)),
            pl.BlockSpec((1, D_MODEL), lambda i: (0, 0)),
            pl.BlockSpec((None, M_WIDTH, D_MODEL), lambda i: (layer, 0, 0), **once),
            pl.BlockSpec((None, G_VWIDTH, D_MODEL), lambda i: (layer, 0, 0), **once),
            pl.BlockSpec((None, D_MODEL, D_MODEL), lambda i: (layer, 0, 0), **once),
        ],
        out_specs=[pl.BlockSpec((tm, D_MODEL), lambda i: (i, 0))] * 2,
        out_shape=[jax.ShapeDtypeStruct((T, D_MODEL), F32),
                   jax.ShapeDtypeStruct((T, D_MODEL), BF16)],
        scratch_shapes=[pltpu.VMEM((tm, M_WIDTH), BF16), pltpu.VMEM((tm, G_VWIDTH), BF16)],
        compiler_params=_cparams(("parallel",)),
        name="mixout",
    )(x, af, ab, bf, bb, z, z, z, z, mgain, ggain, ln2, wpa, wpb, wo)


def _resmm_kernel(x_ref, a_ref, w_ref, out_ref):
    out_ref[...] = x_ref[...] + jnp.dot(a_ref[...], w_ref[...], preferred_element_type=F32)


def _resmm(x, a, w, layer, *, tm=1024, tn=512, name="resmm"):
    T, K = a.shape
    N = w.shape[2]
    return pl.pallas_call(
        _resmm_kernel,
        grid=(T // tm, N // tn),
        in_specs=[
            pl.BlockSpec((tm, tn), lambda i, j: (i, j)),
            pl.BlockSpec((tm, K), lambda i, j: (i, 0)),
            pl.BlockSpec((None, K, tn), lambda i, j: (layer, 0, j)),
        ],
        out_specs=pl.BlockSpec((tm, tn), lambda i, j: (i, j)),
        out_shape=jax.ShapeDtypeStruct((T, N), F32),
        compiler_params=_cparams(("parallel", "arbitrary")),
        name=name,
    )(x, a, w)


def _ffn_up_kernel(xn_ref, wg_ref, wu_ref, h_ref):
    xn = xn_ref[...]
    gate = jnp.dot(xn, wg_ref[...], preferred_element_type=F32)
    up = jnp.dot(xn, wu_ref[...], preferred_element_type=F32)
    h_ref[...] = (gate * _sigmoid(gate) * up).astype(BF16)


def _ffn_up(xn, wgu, layer, *, tm=1024, tn=512):
    T = xn.shape[0]
    nj = D_FF // tn
    return pl.pallas_call(
        _ffn_up_kernel,
        grid=(T // tm, nj),
        in_specs=[
            pl.BlockSpec((tm, D_MODEL), lambda i, j: (i, 0)),
            pl.BlockSpec((None, D_MODEL, tn), lambda i, j: (layer, 0, j)),
            pl.BlockSpec((None, D_MODEL, tn), lambda i, j: (layer, 0, nj + j)),
        ],
        out_specs=pl.BlockSpec((tm, tn), lambda i, j: (i, j)),
        out_shape=jax.ShapeDtypeStruct((T, D_FF), BF16),
        compiler_params=_cparams(("parallel", "arbitrary")),
        name="ffn_up",
    )(xn, wgu, wgu)


def _final_norm_kernel(x_ref, g_ref, out_ref, *, nbatch):
    for bi in range(nbatch):
        out_ref[bi] = _rmsnorm(x_ref[bi * CHUNK:(bi + 1) * CHUNK, :], g_ref[...])


def _final_norm(x, g, blk_off, nbatch, seq):
    nblk = seq // CHUNK
    return pl.pallas_call(
        functools.partial(_final_norm_kernel, nbatch=nbatch),
        grid=(nblk,),
        in_specs=[
            pl.BlockSpec((nbatch * CHUNK, D_MODEL), lambda i: (blk_off + i, 0)),
            pl.BlockSpec((1, D_MODEL), lambda i: (0, 0)),
        ],
        out_specs=pl.BlockSpec((nbatch, CHUNK, D_MODEL), lambda i: (0, i, 0)),
        out_shape=jax.ShapeDtypeStruct((nbatch, seq, D_MODEL), F32),
        compiler_params=_cparams(("parallel",)),
        name="final_norm",
    )(x, g)


def _lane_scan(x, op, fill, reverse):
    n = x.shape[1]
    lane = lax.broadcasted_iota(jnp.int32, x.shape, 1)
    sh = 1
    while sh < n:
        if reverse:
            x = op(x, jnp.where(lane < n - sh, pltpu.roll(x, n - sh, axis=1), fill))
        else:
            x = op(x, jnp.where(lane >= sh, pltpu.roll(x, sh, axis=1), fill))
        sh *= 2
    return x


def _to_column(row, eye):
    return jnp.sum(jnp.where(eye, row, 0.0), axis=1, keepdims=True)


def _mlstm_kernel(qf_ref, kf_ref, vf_ref, gtf_ref, qb_ref, kb_ref, vb_ref, gtb_ref,
                  hf_ref, hb_ref, c_scr, n_scr, m_scr, *, nbatch, n_reset):
    L = CHUNK
    dh = M_HEAD_DIM
    step = pl.program_id(0)

    @pl.when((step == 0) | (step == n_reset))
    def _():
        c_scr[...] = jnp.zeros(c_scr.shape, F32)
        n_scr[...] = jnp.zeros(n_scr.shape, F32)
        m_scr[...] = jnp.full(m_scr.shape, NEG, F32)

    rr = lax.broadcasted_iota(jnp.int32, (L, L), 0)
    cc = lax.broadcasted_iota(jnp.int32, (L, L), 1)
    eye = rr == cc

    dirs = ((qf_ref, kf_ref, vf_ref, gtf_ref, hf_ref), (qb_ref, kb_ref, vb_ref, gtb_ref, hb_ref))
    for d, (q_ref, k_ref, v_ref, gt_ref, out_ref) in enumerate(dirs):
        reverse = d == 1
        mask = (cc >= rr) if reverse else (cc <= rr)
        for bi in range(nbatch):
            rows = slice(bi * L, (bi + 1) * L)
            a, b, cm, g, a_tot, g_max = [
                gt_ref[GATE_ROWS + t * SUBLANES:GATE_ROWS + (t + 1) * SUBLANES, rows]
                for t in range(6)]
            m_old = m_scr[d * nbatch + bi]
            m_new = jnp.maximum(a_tot + m_old, g_max)
            big_m = jnp.maximum(m_old, cm)
            e_int = jnp.exp(m_old - big_m)
            floor = jnp.exp(-(a + big_m))
            ksc = jnp.exp(g - m_new)
            decay = jnp.exp(a_tot + m_old - m_new)
            m_scr[d * nbatch + bi] = m_new

            for h in range(M_HEADS):
                r = d * M_HEADS + h
                ci = (d * nbatch + bi) * M_HEADS + h
                sl = slice(h * dh, (h + 1) * dh)
                m_c = _to_column(big_m[r:r + 1, :], eye)
                e_c = _to_column(e_int[r:r + 1, :], eye)
                f_c = _to_column(floor[r:r + 1, :], eye)
                k_c = _to_column(ksc[r:r + 1, :], eye)
                dec = jnp.concatenate([decay[r:r + 1, :]] * (dh // L), axis=1)

                q = q_ref[rows, sl] * (dh ** -0.5)
                k = k_ref[rows, sl]
                v_b = v_ref[rows, sl].astype(BF16)
                q_b = q.astype(BF16)
                c_old = c_scr[ci]
                n_old = n_scr[ci]

                qk = lax.dot_general(q_b, k.astype(BF16), (((1,), (1,)), ((), ())),
                                     preferred_element_type=F32)
                p = jnp.exp(jnp.where(mask, b[r:r + 1, :] - m_c, NEG))
                s = qk * p
                qn = jnp.sum(q * n_old, axis=1, keepdims=True)
                den = jnp.sum(s, axis=1, keepdims=True) + e_c * qn
                num = (jnp.dot(s.astype(BF16), v_b, preferred_element_type=F32)
                       + e_c * jnp.dot(q_b, c_old.astype(BF16), preferred_element_type=F32))
                out_ref[rows, sl] = num / jnp.maximum(jnp.abs(den), f_c)

                kw = k * k_c
                c_scr[ci] = dec * c_old + lax.dot_general(
                    kw.astype(BF16), v_b, (((0,), (0,)), ((), ())), preferred_element_type=F32)
                n_scr[ci] = dec * n_old + jnp.sum(kw, axis=0, keepdims=True)


def _scan_blocks(n_first, n_total):
    def fwd(n):
        return n

    def bwd(n):
        return jnp.where(n < n_first, n_first - 1 - n, n_first + n_total - 1 - n)

    return fwd, bwd


def _mlstm(z, zst, nbatch, n_first, n_total):
    T = z.shape[0]
    R = nbatch * CHUNK
    fwd, bwd = _scan_blocks(n_first, n_total)
    in_specs = []
    for blk in (fwd, bwd):
        in_specs += [
            pl.BlockSpec((R, M_WIDTH), lambda n, blk=blk: (blk(n), COL_BLOCK["mq"])),
            pl.BlockSpec((R, M_WIDTH), lambda n, blk=blk: (blk(n), COL_BLOCK["mk"])),
            pl.BlockSpec((R, M_WIDTH), lambda n, blk=blk: (blk(n), COL_BLOCK["mv"])),
            pl.BlockSpec((SMALL_WIDTH, R), lambda n, blk=blk: (0, blk(n))),
        ]
    nchain = 2 * nbatch * M_HEADS
    return pl.pallas_call(
        functools.partial(_mlstm_kernel, nbatch=nbatch, n_reset=n_first),
        grid=(n_total,),
        in_specs=in_specs,
        out_specs=[
            pl.BlockSpec((R, M_WIDTH), lambda n: (fwd(n), 0)),
            pl.BlockSpec((R, M_WIDTH), lambda n: (bwd(n), 0)),
        ],
        out_shape=[jax.ShapeDtypeStruct((T, M_WIDTH), F32)] * 2,
        scratch_shapes=[
            pltpu.VMEM((nchain, M_HEAD_DIM, M_HEAD_DIM), F32),
            pltpu.VMEM((nchain, 1, M_HEAD_DIM), F32),
            pltpu.VMEM((2 * nbatch, 8, CHUNK), F32),
        ],
        compiler_params=_cparams(("arbitrary",)),
        name="mlstm",
    )(z, z, z, zst, z, z, z, zst)


def _row_scan_sum(x, tri3):
    hi = x.astype(BF16)
    r1 = x - hi.astype(F32)
    mid = r1.astype(BF16)
    lo = (r1 - mid.astype(F32)).astype(BF16)
    return jnp.dot(tri3, jnp.concatenate([hi, mid, lo], axis=0), preferred_element_type=F32)


def _gla_attention(q, k, gcum, reverse, level_masks, eye):
    L, dk = q.shape
    nb = L // SUBLANES
    row = lax.broadcasted_iota(jnp.int32, (L, dk), 0)
    sub = lax.broadcasted_iota(jnp.int32, (nb, SUBLANES, dk), 1)
    g3 = gcum.reshape(nb, SUBLANES, dk)

    def ref_rows(half):
        first = half if reverse else half - 1
        if half >= SUBLANES:
            pieces = [jnp.broadcast_to(gcum[p * 2 * half + first:p * 2 * half + first + 1, :],
                                       (2 * half, dk)) for p in range(L // (2 * half))]
            return jnp.concatenate(pieces, axis=0)
        if half == 1:
            if reverse:
                ref3 = jnp.where((sub & 1) == 0, pltpu.roll(g3, SUBLANES - 1, axis=1), g3)
            else:
                ref3 = jnp.where((sub & 1) == 1, pltpu.roll(g3, 1, axis=1), g3)
            return ref3.reshape(L, dk)
        ref3 = None
        for p in range(SUBLANES // (2 * half)):
            r = p * 2 * half + first
            piece = jnp.broadcast_to(g3[:, r:r + 1, :], g3.shape)
            ref3 = piece if ref3 is None else jnp.where(sub >= p * 2 * half, piece, ref3)
        return ref3.reshape(L, dk)

    att = jnp.where(eye, jnp.sum(q * k, axis=1, keepdims=True), 0.0)
    for lb, lmask in level_masks:
        e = jnp.exp2(-jnp.abs(gcum - ref_rows(1 << lb)))
        q_side = ((row >> lb) & 1) == (0 if reverse else 1)
        x = (jnp.where(q_side, q, k) * e).astype(BF16)
        prod = lax.dot_general(x, x, (((1,), (1,)), ((), ())), preferred_element_type=F32)
        att = jnp.where(lmask, prod, att)
    return att


def _gla_masks(L, reverse):
    rr = lax.broadcasted_iota(jnp.int32, (L, L), 0)
    cc = lax.broadcasted_iota(jnp.int32, (L, L), 1)
    x = rr ^ cc
    causal = (rr < cc) if reverse else (rr > cc)
    levels = [(lb, causal & ((x >> lb) == 1)) for lb in range(L.bit_length() - 2, -1, -1)]
    eye = rr == cc
    tri = jnp.where(causal | eye, 1.0, 0.0).astype(BF16)
    return levels, eye, jnp.concatenate([tri, tri, tri], axis=1)


def _gla_kernel(qf_ref, kf_ref, vf_ref, zsf_ref, qb_ref, kb_ref, vb_ref, zsb_ref,
                wlrf_ref, wlrb_ref, blrf_ref, blrb_ref, of_ref, ob_ref, st_scr,
                *, nbatch, n_reset):
    L = CHUNK
    dk = G_KEY_DIM
    dv = G_VAL_DIM
    step = pl.program_id(0)

    @pl.when((step == 0) | (step == n_reset))
    def _():
        st_scr[...] = jnp.zeros(st_scr.shape, F32)

    dirs = ((qf_ref, kf_ref, vf_ref, zsf_ref, wlrf_ref, blrf_ref, of_ref),
            (qb_ref, kb_ref, vb_ref, zsb_ref, wlrb_ref, blrb_ref, ob_ref))
    for d, (q_ref, k_ref, v_ref, zs_ref, wlr_ref, blr_ref, out_ref) in enumerate(dirs):
        reverse = d == 1
        level_masks, eye, tri3 = _gla_masks(L, reverse)
        for bi in range(nbatch):
            rows = slice(bi * L, (bi + 1) * L)
            pre = jnp.dot(zs_ref[rows, :].astype(BF16), wlr_ref[...],
                          preferred_element_type=F32) + blr_ref[...]
            g_all = _row_scan_sum(_log_sigmoid(pre) * (LOG2E / G_GATE_NORM), tri3)
            for h in range(G_HEADS):
                ci = (d * nbatch + bi) * G_HEADS + h
                ksl = slice(h * dk, (h + 1) * dk)
                vsl = slice(h * dv, (h + 1) * dv)
                gcum = g_all[:, ksl]
                q = q_ref[rows, ksl] * (dk ** -0.5)
                k = k_ref[rows, ksl]
                v_b = v_ref[rows, vsl].astype(BF16)
                st_old = st_scr[ci]

                att = _gla_attention(q, k, gcum, reverse, level_masks, eye)
                g_last = gcum[0:1, :] if reverse else gcum[L - 1:L, :]
                qg = (q * jnp.exp2(gcum)).astype(BF16)
                out_ref[rows, vsl] = (
                    jnp.dot(att.astype(BF16), v_b, preferred_element_type=F32)
                    + lax.dot_general(qg, st_old.astype(BF16), (((1,), (1,)), ((), ())),
                                      preferred_element_type=F32))
                kw = (k * jnp.exp2(g_last - gcum)).astype(BF16)
                st_scr[ci] = (jnp.exp2(g_last) * st_old
                              + lax.dot_general(v_b, kw, (((0,), (0,)), ((), ())),
                                                preferred_element_type=F32))


def _gla(z, zs, wlr, blr, nbatch, n_first, n_total):
    T = z.shape[0]
    R = nbatch * CHUNK
    fwd, bwd = _scan_blocks(n_first, n_total)
    in_specs = []
    for blk in (fwd, bwd):
        in_specs += [
            pl.BlockSpec((R, G_KWIDTH), lambda n, blk=blk: (blk(n), COL_BLOCK["gq"])),
            pl.BlockSpec((R, G_KWIDTH), lambda n, blk=blk: (blk(n), COL_BLOCK["gk"])),
            pl.BlockSpec((R, G_VWIDTH), lambda n, blk=blk: (blk(n), COL_BLOCK["gv"])),
            pl.BlockSpec((R, SMALL_WIDTH), lambda n, blk=blk: (blk(n), 0)),
        ]
    in_specs += [pl.BlockSpec((SMALL_WIDTH, G_KWIDTH), lambda n: (0, 0))] * 2
    in_specs += [pl.BlockSpec((1, G_KWIDTH), lambda n: (0, 0))] * 2
    return pl.pallas_call(
        functools.partial(_gla_kernel, nbatch=nbatch, n_reset=n_first),
        grid=(n_total,),
        in_specs=in_specs,
        out_specs=[
            pl.BlockSpec((R, G_VWIDTH), lambda n: (fwd(n), 0)),
            pl.BlockSpec((R, G_VWIDTH), lambda n: (bwd(n), 0)),
        ],
        out_shape=[jax.ShapeDtypeStruct((T, G_VWIDTH), F32)] * 2,
        scratch_shapes=[pltpu.VMEM((2 * nbatch * G_HEADS, G_VAL_DIM, G_KEY_DIM), F32)],
        compiler_params=_cparams(("arbitrary",)),
        name="gla",
    )(z, z, z, zs, z, z, z, zs, wlr[0], wlr[1], blr[0], blr[1])


def _split_cols(t):
    o_mi = 4 * M_WIDTH
    o_gq = o_mi + 4 * M_HEADS
    o_lr = o_gq + 2 * G_KWIDTH + 2 * G_VWIDTH
    o_ga = o_lr + 2 * G_LOWRANK
    big = jnp.concatenate([t[..., o_ga:], t[..., :o_mi], t[..., o_gq:o_lr]], axis=-1)
    small = jnp.concatenate([t[..., o_mi:o_gq], t[..., o_lr:o_ga]], axis=-1)
    pad = [(0, 0)] * (t.ndim - 1) + [(0, SMALL_WIDTH - GATE_ROWS)]
    return big, jnp.pad(small, pad)


def _low_rank_weights(w_lr2):
    out = []
    for d in range(2):
        lo = 4 * M_HEADS + d * G_LOWRANK
        out.append(jnp.zeros((SMALL_WIDTH, G_KWIDTH), F32).at[lo:lo + G_LOWRANK].set(w_lr2[d])
                   .astype(BF16))
    return out


def _interleave(x):
    B, S, D = x.shape
    return x.reshape(B, S // CHUNK, CHUNK, D).transpose(1, 0, 2, 3).reshape(B * S, D)


def kernel(x_prompt, x_sample, ln1, w_in, b_in, m_norm, w_lr2, b_lr2, g_norm, w_pa, w_pb, w_o,
           ln2, w_gu, w_down, ln_f):
    bp, sp, _ = x_prompt.shape
    bs, ss, _ = x_sample.shape
    assert bp == bs, "both request groups must have the same batch size"
    nb = bs
    n_first = ss // CHUNK
    n_total = n_first + sp // CHUNK
    x = jnp.concatenate([_interleave(x_sample), _interleave(x_prompt)], axis=0)
    scan_mat = _scan_matrix()

    w_big, w_small = _split_cols(w_in)
    b_big, b_small = _split_cols(b_in)
    w_big = w_big.astype(BF16)
    w_small = w_small.astype(BF16)
    w_pa, w_pb, w_o, w_gu, w_down = (t.astype(BF16) for t in (w_pa, w_pb, w_o, w_gu, w_down))

    for l in range(ln1.shape[0]):
        wlr = _low_rank_weights(w_lr2[l])
        blr = [b_lr2[l, 0][None, :], b_lr2[l, 1][None, :]]
        z, zs, zst = _inproj(x, ln1[l][None, :], w_big, l, b_big[l][None, :], w_small[l],
                             b_small[l][None, :], w_small[l].T, b_small[l][:, None], scan_mat)
        af, ab = _mlstm(z, zst, nb, n_first, n_total)
        bf, bb = _gla(z, zs, wlr, blr, nb, n_first, n_total)
        x, xn = _mixout(x, af, ab, bf, bb, z, m_norm[l][None, :], g_norm[l][None, :],
                        ln2[l][None, :], w_pa, w_pb, w_o, l)
        hmid = _ffn_up(xn, w_gu, l)
        x = _resmm(x, hmid, w_down, l, name="ffn_down")

    g_f = ln_f[None, :]
    return (_final_norm(x, g_f, n_first, nb, sp), _final_norm(x, g_f, 0, nb, ss))
```

```python
import functools

import jax
import jax.numpy as jnp
from jax import lax
from jax.experimental import pallas as pl
from jax.experimental.pallas import tpu as pltpu

F32 = jnp.float32
BF16 = jnp.bfloat16

D_MODEL = 2048
M_HEADS = 4
M_HEAD_DIM = 256
M_WIDTH = 1024
G_HEADS = 4
G_VAL_DIM = 256
G_KEY_DIM = 128
G_VWIDTH = 1024
G_KWIDTH = 512
G_LOWRANK = 16
G_GATE_NORM = 16.0
D_FF = 5632
EPS = 1e-6
NEG = -1e30
LOG2E = 1.4426950408889634

_GROUPS = (("ga", D_MODEL), ("gb", D_MODEL), ("mq", M_WIDTH), ("mk", M_WIDTH), ("mv", M_WIDTH),
           ("mo", M_WIDTH), ("gq", G_KWIDTH), ("gk", G_KWIDTH), ("gv", G_VWIDTH), ("gg", G_VWIDTH))
COL_BLOCK = {}
BIG_WIDTH = 0
for _name, _w in _GROUPS:
    assert BIG_WIDTH % _w == 0
    COL_BLOCK[_name] = BIG_WIDTH // _w
    BIG_WIDTH += _w
SMALL_WIDTH = 128
GATE_ROWS = 4 * M_HEADS + 2 * G_LOWRANK
CHUNK = 128
SUBLANES = 8

VMEM_LIMIT = 56 * 1024 * 1024


def _cparams(sem):
    return pltpu.CompilerParams(dimension_semantics=sem, vmem_limit_bytes=VMEM_LIMIT)


def _log_sigmoid(x):
    return jnp.minimum(x, 0.0) - jnp.log1p(jnp.exp(-jnp.abs(x)))


def _sigmoid(x):
    return 1.0 / (1.0 + jnp.exp(-x))


def _rmsnorm(x, g):
    return x * lax.rsqrt(jnp.mean(x * x, axis=-1, keepdims=True) + EPS) * g


def _inproj_kernel(x_ref, g_ref, w_ref, b_ref, ws_ref, bs_ref, scan_ref,
                   z_ref, zs_ref, zst_ref, xn_ref, *, n_col):
    i = pl.program_id(0)
    j = pl.program_id(1)
    last = n_col - 1
    cur = i % 2

    def matmul():
        z_ref[...] = jnp.dot(xn_ref[cur], w_ref[...], preferred_element_type=F32) + b_ref[...]

    def normalise(slot):
        xn_ref[slot] = _rmsnorm(x_ref[...], g_ref[...]).astype(BF16)

    def small_outputs():
        xn = xn_ref[cur]
        zs = jnp.dot(xn, ws_ref[...], preferred_element_type=F32) + bs_ref[...]
        zs_ref[...] = zs
        zt = zs.T
        n_derived = 6 * SUBLANES
        zst_ref[0:GATE_ROWS, :] = zt[0:GATE_ROWS, :]
        zst_ref[GATE_ROWS + n_derived:, :] = zt[GATE_ROWS + n_derived:, :]
        nchunk = zst_ref.shape[1] // CHUNK
        chunks = [slice(c * CHUNK, (c + 1) * CHUNK) for c in range(nchunk)]
        i_pre = jnp.concatenate([zt[0:8, cols] for cols in chunks], axis=0)
        lf = _log_sigmoid(jnp.concatenate([zt[8:16, cols] for cols in chunks], axis=0))
        sub = lax.broadcasted_iota(jnp.int32, lf.shape, 0)
        fwd_row = (sub & (SUBLANES - 1)) < M_HEADS
        hi = lf.astype(BF16)
        r1 = lf - hi.astype(F32)
        mid = r1.astype(BF16)
        lo3 = (r1 - mid.astype(F32)).astype(BF16)
        sums = jnp.dot(jnp.concatenate([hi, mid, lo3], axis=1), scan_ref[...],
                       preferred_element_type=F32)
        a = jnp.where(fwd_row, sums[:, 0:CHUNK], sums[:, CHUNK:2 * CHUNK])
        a_tot = sums[:, 2 * CHUNK:3 * CHUNK]
        b = i_pre - a
        g = a_tot - a + i_pre
        cm = jnp.where(fwd_row, _lane_scan(b, jnp.maximum, -jnp.inf, False),
                       _lane_scan(b, jnp.maximum, -jnp.inf, True))
        g_max = jnp.broadcast_to(jnp.max(g, axis=1, keepdims=True), g.shape)
        for slot, val in enumerate((a, b, cm, g, a_tot, g_max)):
            lo = GATE_ROWS + slot * SUBLANES
            for c, cols in enumerate(chunks):
                zst_ref[lo:lo + SUBLANES, cols] = val[c * SUBLANES:(c + 1) * SUBLANES, :]

    @pl.when((i == 0) & (j == 0))
    def _():
        normalise(0)

    @pl.when(j == 1)
    def _():
        small_outputs()
        matmul()

    @pl.when(j == last)
    def _():
        normalise(1 - cur)
        matmul()

    @pl.when((j != 1) & (j != last))
    def _():
        matmul()


def _scan_matrix():
    s = jnp.arange(CHUNK)[:, None]
    t = jnp.arange(CHUNK)[None, :]
    one = jnp.concatenate([s <= t, s >= t, jnp.ones((CHUNK, CHUNK), bool)], axis=1).astype(BF16)
    return jnp.concatenate([one, one, one], axis=0)


def _inproj(x, g, w, layer, b, ws, bs, scan_mat, *, tm=1024, tn=1024):
    T = x.shape[0]
    N = w.shape[2]
    n_row, n_col = T // tm, N // tn
    assert n_col >= 3, "needs distinct first, second and last column steps"

    def x_block(i, j):
        return (jnp.minimum(i + (j == n_col - 1).astype(jnp.int32), n_row - 1), 0)

    return pl.pallas_call(
        functools.partial(_inproj_kernel, n_col=n_col),
        grid=(n_row, n_col),
        in_specs=[
            pl.BlockSpec((tm, D_MODEL), x_block),
            pl.BlockSpec((1, D_MODEL), lambda i, j: (0, 0)),
            pl.BlockSpec((None, D_MODEL, tn), lambda i, j: (layer, 0, j)),
            pl.BlockSpec((1, tn), lambda i, j: (0, j)),
            pl.BlockSpec((D_MODEL, SMALL_WIDTH), lambda i, j: (0, 0)),
            pl.BlockSpec((1, SMALL_WIDTH), lambda i, j: (0, 0)),
            pl.BlockSpec((3 * CHUNK, 3 * CHUNK), lambda i, j: (0, 0)),
        ],
        out_specs=[
            pl.BlockSpec((tm, tn), lambda i, j: (i, j)),
            pl.BlockSpec((tm, SMALL_WIDTH), lambda i, j: (i, 0)),
            pl.BlockSpec((SMALL_WIDTH, tm), lambda i, j: (0, i)),
        ],
        out_shape=[
            jax.ShapeDtypeStruct((T, N), F32),
            jax.ShapeDtypeStruct((T, SMALL_WIDTH), F32),
            jax.ShapeDtypeStruct((SMALL_WIDTH, T), F32),
        ],
        scratch_shapes=[pltpu.VMEM((2, tm, D_MODEL), BF16)],
        compiler_params=_cparams(("arbitrary", "arbitrary")),
        name="inproj",
    )(x, g, w, b, ws, bs, scan_mat)


def _head_norm_gate(hsum, gain, gate):
    y = hsum * lax.rsqrt(jnp.mean(hsum * hsum, axis=-1, keepdims=True) + EPS)
    return (y * gain * gate).astype(BF16)


def _mixout_kernel(x_ref, af_ref, ab_ref, bf_ref, bb_ref, mo_ref, gg_ref, ga_ref, gb_ref,
                   mgain_ref, ggain_ref, ln2_ref, wpa_ref, wpb_ref, wo_ref, out_ref, xn_ref,
                   ha_scr, hb_scr):
    for h in range(M_HEADS):
        sl = slice(h * M_HEAD_DIM, (h + 1) * M_HEAD_DIM)
        ha_scr[:, sl] = _head_norm_gate(af_ref[:, sl] + ab_ref[:, sl], mgain_ref[:, sl],
                                        _sigmoid(mo_ref[:, sl]))
    for h in range(G_HEADS):
        sl = slice(h * G_VAL_DIM, (h + 1) * G_VAL_DIM)
        gg = gg_ref[:, sl]
        hb_scr[:, sl] = _head_norm_gate(bf_ref[:, sl] + bb_ref[:, sl], ggain_ref[:, sl],
                                        gg * _sigmoid(gg))
    pa = jnp.dot(ha_scr[...], wpa_ref[...], preferred_element_type=F32)
    pb = jnp.dot(hb_scr[...], wpb_ref[...], preferred_element_type=F32)
    merged = (_sigmoid(ga_ref[...]) * pa + _sigmoid(gb_ref[...]) * pb).astype(BF16)
    x_new = x_ref[...] + jnp.dot(merged, wo_ref[...], preferred_element_type=F32)
    out_ref[...] = x_new
    xn_ref[...] = _rmsnorm(x_new, ln2_ref[...]).astype(BF16)


def _mixout(x, af, ab, bf, bb, z, mgain, ggain, ln2, wpa, wpb, wo, layer, *, tm=256):
    T = x.shape[0]
    mo_blk, gg_blk, ga_blk, gb_blk = (COL_BLOCK[n] for n in ("mo", "gg", "ga", "gb"))
    once = dict(pipeline_mode=pl.Buffered(1))
    return pl.pallas_call(
        _mixout_kernel,
        grid=(T // tm,),
        in_specs=[
            pl.BlockSpec((tm, D_MODEL), lambda i: (i, 0)),
            pl.BlockSpec((tm, M_WIDTH), lambda i: (i, 0)),
            pl.BlockSpec((tm, M_WIDTH), lambda i: (i, 0)),
            pl.BlockSpec((tm, G_VWIDTH), lambda i: (i, 0)),
            pl.BlockSpec((tm, G_VWIDTH), lambda i: (i, 0)),
            pl.BlockSpec((tm, M_WIDTH), lambda i: (i, mo_blk)),
            pl.BlockSpec((tm, G_VWIDTH), lambda i: (i, gg_blk)),
            pl.BlockSpec((tm, D_MODEL), lambda i: (i, ga_blk)),
            pl.BlockSpec((tm, D_MODEL), lambda i: (i, gb_blk)),
            pl.BlockSpec((1, M_WIDTH), lambda i: (0, 0)),
            pl.BlockSpec((1, G_VWIDTH), lambda i: (0, 0)),
            pl.BlockSpec((1, D_MODEL), lambda i: (0, 0)),
            pl.BlockSpec((None, M_WIDTH, D_MODEL), lambda i: (layer, 0, 0), **once),
            pl.BlockSpec((None, G_VWIDTH, D_MODEL), lambda i: (layer, 0, 0), **once),
            pl.BlockSpec((None, D_MODEL, D_MODEL), lambda i: (layer, 0, 0), **once),
        ],
        out_specs=[pl.BlockSpec((tm, D_MODEL), lambda i: (i, 0))] * 2,
        out_shape=[jax.ShapeDtypeStruct((T, D_MODEL), F32),
                   jax.ShapeDtypeStruct((T, D_MODEL), BF16)],
        scratch_shapes=[pltpu.VMEM((tm, M_WIDTH), BF16), pltpu.VMEM((tm, G_VWIDTH), BF16)],
        compiler_params=_cparams(("parallel",)),
        name="mixout",
    )(x, af, ab, bf, bb, z, z, z, z, mgain, ggain, ln2, wpa, wpb, wo)


def _resmm_kernel(x_ref, a_ref, w_ref, out_ref):
    out_ref[...] = x_ref[...] + jnp.dot(a_ref[...], w_ref[...], preferred_element_type=F32)


def _resmm(x, a, w, layer, *, tm=1024, tn=512, name="resmm"):
    T, K = a.shape
    N = w.shape[2]
    return pl.pallas_call(
        _resmm_kernel,
        grid=(T // tm, N // tn),
        in_specs=[
            pl.BlockSpec((tm, tn), lambda i, j: (i, j)),
            pl.BlockSpec((tm, K), lambda i, j: (i, 0)),
            pl.BlockSpec((None, K, tn), lambda i, j: (layer, 0, j)),
        ],
        out_specs=pl.BlockSpec((tm, tn), lambda i, j: (i, j)),
        out_shape=jax.ShapeDtypeStruct((T, N), F32),
        compiler_params=_cparams(("parallel", "arbitrary")),
        name=name,
    )(x, a, w)


def _ffn_up_kernel(xn_ref, wg_ref, wu_ref, h_ref):
    xn = xn_ref[...]
    gate = jnp.dot(xn, wg_ref[...], preferred_element_type=F32)
    up = jnp.dot(xn, wu_ref[...], preferred_element_type=F32)
    h_ref[...] = (gate * _sigmoid(gate) * up).astype(BF16)


def _ffn_up(xn, wgu, layer, *, tm=1024, tn=512):
    T = xn.shape[0]
    nj = D_FF // tn
    return pl.pallas_call(
        _ffn_up_kernel,
        grid=(T // tm, nj),
        in_specs=[
            pl.BlockSpec((tm, D_MODEL), lambda i, j: (i, 0)),
            pl.BlockSpec((None, D_MODEL, tn), lambda i, j: (layer, 0, j)),
            pl.BlockSpec((None, D_MODEL, tn), lambda i, j: (layer, 0, nj + j)),
        ],
        out_specs=pl.BlockSpec((tm, tn), lambda i, j: (i, j)),
        out_shape=jax.ShapeDtypeStruct((T, D_FF), BF16),
        compiler_params=_cparams(("parallel", "arbitrary")),
        name="ffn_up",
    )(xn, wgu, wgu)


def _final_norm_kernel(x_ref, g_ref, out_ref, *, nbatch):
    for bi in range(nbatch):
        out_ref[bi] = _rmsnorm(x_ref[bi * CHUNK:(bi + 1) * CHUNK, :], g_ref[...])


def _final_norm(x, g, blk_off, nbatch, seq):
    nblk = seq // CHUNK
    return pl.pallas_call(
        functools.partial(_final_norm_kernel, nbatch=nbatch),
        grid=(nblk,),
        in_specs=[
            pl.BlockSpec((nbatch * CHUNK, D_MODEL), lambda i: (blk_off + i, 0)),
            pl.BlockSpec((1, D_MODEL), lambda i: (0, 0)),
        ],
        out_specs=pl.BlockSpec((nbatch, CHUNK, D_MODEL), lambda i: (0, i, 0)),
        out_shape=jax.ShapeDtypeStruct((nbatch, seq, D_MODEL), F32),
        compiler_params=_cparams(("parallel",)),
        name="final_norm",
    )(x, g)


def _lane_scan(x, op, fill, reverse):
    n = x.shape[1]
    lane = lax.broadcasted_iota(jnp.int32, x.shape, 1)
    sh = 1
    while sh < n:
        if reverse:
            x = op(x, jnp.where(lane < n - sh, pltpu.roll(x, n - sh, axis=1), fill))
        else:
            x = op(x, jnp.where(lane >= sh, pltpu.roll(x, sh, axis=1), fill))
        sh *= 2
    return x


def _to_column(row, eye):
    return jnp.sum(jnp.where(eye, row, 0.0), axis=1, keepdims=True)


def _mlstm_kernel(qf_ref, kf_ref, vf_ref, gtf_ref, qb_ref, kb_ref, vb_ref, gtb_ref,
                  hf_ref, hb_ref, c_scr, n_scr, m_scr, *, nbatch, n_reset):
    L = CHUNK
    dh = M_HEAD_DIM
    step = pl.program_id(0)

    @pl.when((step == 0) | (step == n_reset))
    def _():
        c_scr[...] = jnp.zeros(c_scr.shape, F32)
        n_scr[...] = jnp.zeros(n_scr.shape, F32)
        m_scr[...] = jnp.full(m_scr.shape, NEG, F32)

    rr = lax.broadcasted_iota(jnp.int32, (L, L), 0)
    cc = lax.broadcasted_iota(jnp.int32, (L, L), 1)
    eye = rr == cc

    dirs = ((qf_ref, kf_ref, vf_ref, gtf_ref, hf_ref), (qb_ref, kb_ref, vb_ref, gtb_ref, hb_ref))
    for d, (q_ref, k_ref, v_ref, gt_ref, out_ref) in enumerate(dirs):
        reverse = d == 1
        mask = (cc >= rr) if reverse else (cc <= rr)
        for bi in range(nbatch):
            rows = slice(bi * L, (bi + 1) * L)
            a, b, cm, g, a_tot, g_max = [
                gt_ref[GATE_ROWS + t * SUBLANES:GATE_ROWS + (t + 1) * SUBLANES, rows]
                for t in range(6)]
            m_old = m_scr[d * nbatch + bi]
            m_new = jnp.maximum(a_tot + m_old, g_max)
            big_m = jnp.maximum(m_old, cm)
            e_int = jnp.exp(m_old - big_m)
            floor = jnp.exp(-(a + big_m))
            ksc = jnp.exp(g - m_new)
            decay = jnp.exp(a_tot + m_old - m_new)
            m_scr[d * nbatch + bi] = m_new

            for h in range(M_HEADS):
                r = d * M_HEADS + h
                ci = (d * nbatch + bi) * M_HEADS + h
                sl = slice(h * dh, (h + 1) * dh)
                m_c = _to_column(big_m[r:r + 1, :], eye)
                e_c = _to_column(e_int[r:r + 1, :], eye)
                f_c = _to_column(floor[r:r + 1, :], eye)
                k_c = _to_column(ksc[r:r + 1, :], eye)
                dec = jnp.concatenate([decay[r:r + 1, :]] * (dh // L), axis=1)

                q = q_ref[rows, sl] * (dh ** -0.5)
                k = k_ref[rows, sl]
                v_b = v_ref[rows, sl].astype(BF16)
                q_b = q.astype(BF16)
                c_old = c_scr[ci]
                n_old = n_scr[ci]

                qk = lax.dot_general(q_b, k.astype(BF16), (((1,), (1,)), ((), ())),
                                     preferred_element_type=F32)
                p = jnp.exp(jnp.where(mask, b[r:r + 1, :] - m_c, NEG))
                s = qk * p
                qn = jnp.sum(q * n_old, axis=1, keepdims=True)
                den = jnp.sum(s, axis=1, keepdims=True) + e_c * qn
                num = (jnp.dot(s.astype(BF16), v_b, preferred_element_type=F32)
                       + e_c * jnp.dot(q_b, c_old.astype(BF16), preferred_element_type=F32))
                out_ref[rows, sl] = num / jnp.maximum(jnp.abs(den), f_c)

                kw = k * k_c
                c_scr[ci] = dec * c_old + lax.dot_general(
                    kw.astype(BF16), v_b, (((0,), (0,)), ((), ())), preferred_element_type=F32)
                n_scr[ci] = dec * n_old + jnp.sum(kw, axis=0, keepdims=True)


def _scan_blocks(n_first, n_total):
    def fwd(n):
        return n

    def bwd(n):
        return jnp.where(n < n_first, n_first - 1 - n, n_first + n_total - 1 - n)

    return fwd, bwd


def _mlstm(z, zst, nbatch, n_first, n_total):
    T = z.shape[0]
    R = nbatch * CHUNK
    fwd, bwd = _scan_blocks(n_first, n_total)
    in_specs = []
    for blk in (fwd, bwd):
        in_specs += [
            pl.BlockSpec((R, M_WIDTH), lambda n, blk=blk: (blk(n), COL_BLOCK["mq"])),
            pl.BlockSpec((R, M_WIDTH), lambda n, blk=blk: (blk(n), COL_BLOCK["mk"])),
            pl.BlockSpec((R, M_WIDTH), lambda n, blk=blk: (blk(n), COL_BLOCK["mv"])),
            pl.BlockSpec((SMALL_WIDTH, R), lambda n, blk=blk: (0, blk(n))),
        ]
    nchain = 2 * nbatch * M_HEADS
    return pl.pallas_call(
        functools.partial(_mlstm_kernel, nbatch=nbatch, n_reset=n_first),
        grid=(n_total,),
        in_specs=in_specs,
        out_specs=[
            pl.BlockSpec((R, M_WIDTH), lambda n: (fwd(n), 0)),
            pl.BlockSpec((R, M_WIDTH), lambda n: (bwd(n), 0)),
        ],
        out_shape=[jax.ShapeDtypeStruct((T, M_WIDTH), F32)] * 2,
        scratch_shapes=[
            pltpu.VMEM((nchain, M_HEAD_DIM, M_HEAD_DIM), F32),
            pltpu.VMEM((nchain, 1, M_HEAD_DIM), F32),
            pltpu.VMEM((2 * nbatch, 8, CHUNK), F32),
        ],
        compiler_params=_cparams(("arbitrary",)),
        name="mlstm",
    )(z, z, z, zst, z, z, z, zst)


def _row_scan_sum(x, tri3):
    hi = x.astype(BF16)
    r1 = x - hi.astype(F32)
    mid = r1.astype(BF16)
    lo = (r1 - mid.astype(F32)).astype(BF16)
    return jnp.dot(tri3, jnp.concatenate([hi, mid, lo], axis=0), preferred_element_type=F32)


def _gla_attention(q, k, gcum, reverse, level_masks, eye):
    L, dk = q.shape
    nb = L // SUBLANES
    row = lax.broadcasted_iota(jnp.int32, (L, dk), 0)
    sub = lax.broadcasted_iota(jnp.int32, (nb, SUBLANES, dk), 1)
    g3 = gcum.reshape(nb, SUBLANES, dk)

    def ref_rows(half):
        first = half if reverse else half - 1
        if half >= SUBLANES:
            pieces = [jnp.broadcast_to(gcum[p * 2 * half + first:p * 2 * half + first + 1, :],
                                       (2 * half, dk)) for p in range(L // (2 * half))]
            return jnp.concatenate(pieces, axis=0)
        if half == 1:
            if reverse:
                ref3 = jnp.where((sub & 1) == 0, pltpu.roll(g3, SUBLANES - 1, axis=1), g3)
            else:
                ref3 = jnp.where((sub & 1) == 1, pltpu.roll(g3, 1, axis=1), g3)
            return ref3.reshape(L, dk)
        ref3 = None
        for p in range(SUBLANES // (2 * half)):
            r = p * 2 * half + first
            piece = jnp.broadcast_to(g3[:, r:r + 1, :], g3.shape)
            ref3 = piece if ref3 is None else jnp.where(sub >= p * 2 * half, piece, ref3)
        return ref3.reshape(L, dk)

    att = jnp.where(eye, jnp.sum(q * k, axis=1, keepdims=True), 0.0)
    for lb, lmask in level_masks:
        e = jnp.exp2(-jnp.abs(gcum - ref_rows(1 << lb)))
        q_side = ((row >> lb) & 1) == (0 if reverse else 1)
        x = (jnp.where(q_side, q, k) * e).astype(BF16)
        prod = lax.dot_general(x, x, (((1,), (1,)), ((), ())), preferred_element_type=F32)
        att = jnp.where(lmask, prod, att)
    return att


def _gla_masks(L, reverse):
    rr = lax.broadcasted_iota(jnp.int32, (L, L), 0)
    cc = lax.broadcasted_iota(jnp.int32, (L, L), 1)
    x = rr ^ cc
    causal = (rr < cc) if reverse else (rr > cc)
    levels = [(lb, causal & ((x >> lb) == 1)) for lb in range(L.bit_length() - 2, -1, -1)]
    eye = rr == cc
    tri = jnp.where(causal | eye, 1.0, 0.0).astype(BF16)
    return levels, eye, jnp.concatenate([tri, tri, tri], axis=1)


def _gla_kernel(qf_ref, kf_ref, vf_ref, zsf_ref, qb_ref, kb_ref, vb_ref, zsb_ref,
                wlrf_ref, wlrb_ref, blrf_ref, blrb_ref, of_ref, ob_ref, st_scr,
                *, nbatch, n_reset):
    L = CHUNK
    dk = G_KEY_DIM
    dv = G_VAL_DIM
    step = pl.program_id(0)

    @pl.when((step == 0) | (step == n_reset))
    def _():
        st_scr[...] = jnp.zeros(st_scr.shape, F32)

    dirs = ((qf_ref, kf_ref, vf_ref, zsf_ref, wlrf_ref, blrf_ref, of_ref),
            (qb_ref, kb_ref, vb_ref, zsb_ref, wlrb_ref, blrb_ref, ob_ref))
    for d, (q_ref, k_ref, v_ref, zs_ref, wlr_ref, blr_ref, out_ref) in enumerate(dirs):
        reverse = d == 1
        level_masks, eye, tri3 = _gla_masks(L, reverse)
        for bi in range(nbatch):
            rows = slice(bi * L, (bi + 1) * L)
            pre = jnp.dot(zs_ref[rows, :].astype(BF16), wlr_ref[...],
                          preferred_element_type=F32) + blr_ref[...]
            g_all = _row_scan_sum(_log_sigmoid(pre) * (LOG2E / G_GATE_NORM), tri3)
            for h in range(G_HEADS):
                ci = (d * nbatch + bi) * G_HEADS + h
                ksl = slice(h * dk, (h + 1) * dk)
                vsl = slice(h * dv, (h + 1) * dv)
                gcum = g_all[:, ksl]
                q = q_ref[rows, ksl] * (dk ** -0.5)
                k = k_ref[rows, ksl]
                v_b = v_ref[rows, vsl].astype(BF16)
                st_old = st_scr[ci]

                att = _gla_attention(q, k, gcum, reverse, level_masks, eye)
                g_last = gcum[0:1, :] if reverse else gcum[L - 1:L, :]
                qg = (q * jnp.exp2(gcum)).astype(BF16)
                out_ref[rows, vsl] = (
                    jnp.dot(att.astype(BF16), v_b, preferred_element_type=F32)
                    + lax.dot_general(qg, st_old.astype(BF16), (((1,), (1,)), ((), ())),
                                      preferred_element_type=F32))
                kw = (k * jnp.exp2(g_last - gcum)).astype(BF16)
                st_scr[ci] = (jnp.exp2(g_last) * st_old
                              + lax.dot_general(v_b, kw, (((0,), (0,)), ((), ())),
                                                preferred_element_type=F32))


def _gla(z, zs, wlr, blr, nbatch, n_first, n_total):
    T = z.shape[0]
    R = nbatch * CHUNK
    fwd, bwd = _scan_blocks(n_first, n_total)
    in_specs = []
    for blk in (fwd, bwd):
        in_specs += [
            pl.BlockSpec((R, G_KWIDTH), lambda n, blk=blk: (blk(n), COL_BLOCK["gq"])),
            pl.BlockSpec((R, G_KWIDTH), lambda n, blk=blk: (blk(n), COL_BLOCK["gk"])),
            pl.BlockSpec((R, G_VWIDTH), lambda n, blk=blk: (blk(n), COL_BLOCK["gv"])),
            pl.BlockSpec((R, SMALL_WIDTH), lambda n, blk=blk: (blk(n), 0)),
        ]
    in_specs += [pl.BlockSpec((SMALL_WIDTH, G_KWIDTH), lambda n: (0, 0))] * 2
    in_specs += [pl.BlockSpec((1, G_KWIDTH), lambda n: (0, 0))] * 2
    return pl.pallas_call(
        functools.partial(_gla_kernel, nbatch=nbatch, n_reset=n_first),
        grid=(n_total,),
        in_specs=in_specs,
        out_specs=[
            pl.BlockSpec((R, G_VWIDTH), lambda n: (fwd(n), 0)),
            pl.BlockSpec((R, G_VWIDTH), lambda n: (bwd(n), 0)),
        ],
        out_shape=[jax.ShapeDtypeStruct((T, G_VWIDTH), F32)] * 2,
        scratch_shapes=[pltpu.VMEM((2 * nbatch * G_HEADS, G_VAL_DIM, G_KEY_DIM), F32)],
        compiler_params=_cparams(("arbitrary",)),
        name="gla",
    )(z, z, z, zs, z, z, z, zs, wlr[0], wlr[1], blr[0], blr[1])


def _split_cols(t):
    o_mi = 4 * M_WIDTH
    o_gq = o_mi + 4 * M_HEADS
    o_lr = o_gq + 2 * G_KWIDTH + 2 * G_VWIDTH
    o_ga = o_lr + 2 * G_LOWRANK
    big = jnp.concatenate([t[..., o_ga:], t[..., :o_mi], t[..., o_gq:o_lr]], axis=-1)
    small = jnp.concatenate([t[..., o_mi:o_gq], t[..., o_lr:o_ga]], axis=-1)
    pad = [(0, 0)] * (t.ndim - 1) + [(0, SMALL_WIDTH - GATE_ROWS)]
    return big, jnp.pad(small, pad)


def _low_rank_weights(w_lr2):
    out = []
    for d in range(2):
        lo = 4 * M_HEADS + d * G_LOWRANK
        out.append(jnp.zeros((SMALL_WIDTH, G_KWIDTH), F32).at[lo:lo + G_LOWRANK].set(w_lr2[d])
                   .astype(BF16))
    return out


def _interleave_kernel(xs_ref, xp_ref, out_ref, *, nbatch, n_first):
    step = pl.program_id(0)

    @pl.when(step < n_first)
    def _():
        for bi in range(nbatch):
            out_ref[bi * CHUNK:(bi + 1) * CHUNK, :] = xs_ref[bi, 0]

    @pl.when(step >= n_first)
    def _():
        for bi in range(nbatch):
            out_ref[bi * CHUNK:(bi + 1) * CHUNK, :] = xp_ref[bi, 0]


def _interleave(x_first, x_second):
    nb, s1, D = x_first.shape
    s2 = x_second.shape[1]
    n_first = s1 // CHUNK
    n_second = s2 // CHUNK
    blk = (nb, 1, CHUNK, D)
    return pl.pallas_call(
        functools.partial(_interleave_kernel, nbatch=nb, n_first=n_first),
        grid=(n_first + n_second,),
        in_specs=[
            pl.BlockSpec(blk, lambda n: (0, jnp.minimum(n, n_first - 1), 0, 0)),
            pl.BlockSpec(blk, lambda n: (0, jnp.maximum(n - n_first, 0), 0, 0)),
        ],
        out_specs=pl.BlockSpec((nb * CHUNK, D), lambda n: (n, 0)),
        out_shape=jax.ShapeDtypeStruct((nb * (s1 + s2), D), x_first.dtype),
        compiler_params=_cparams(("arbitrary",)),
        name="interleave",
    )(x_first.reshape(nb, n_first, CHUNK, D), x_second.reshape(nb, n_second, CHUNK, D))


def kernel(x_prompt, x_sample, ln1, w_in, b_in, m_norm, w_lr2, b_lr2, g_norm, w_pa, w_pb, w_o,
           ln2, w_gu, w_down, ln_f):
    bp, sp, _ = x_prompt.shape
    bs, ss, _ = x_sample.shape
    assert bp == bs, "both request groups must have the same batch size"
    nb = bs
    n_first = ss // CHUNK
    n_total = n_first + sp // CHUNK
    x = _interleave(x_sample, x_prompt)
    scan_mat = _scan_matrix()

    w_big, w_small = _split_cols(w_in.astype(BF16))
    b_big, b_small = _split_cols(b_in)
    w_pa, w_pb, w_o, w_gu, w_down = (t.astype(BF16) for t in (w_pa, w_pb, w_o, w_gu, w_down))

    for l in range(ln1.shape[0]):
        wlr = _low_rank_weights(w_lr2[l])
        blr = [b_lr2[l, 0][None, :], b_lr2[l, 1][None, :]]
        z, zs, zst = _inproj(x, ln1[l][None, :], w_big, l, b_big[l][None, :], w_small[l],
                             b_small[l][None, :], scan_mat)
        af, ab = _mlstm(z, zst, nb, n_first, n_total)
        bf, bb = _gla(z, zs, wlr, blr, nb, n_first, n_total)
        x, xn = _mixout(x, af, ab, bf, bb, z, m_norm[l][None, :], g_norm[l][None, :],
                        ln2[l][None, :], w_pa, w_pb, w_o, l)
        hmid = _ffn_up(xn, w_gu, l)
        x = _resmm(x, hmid, w_down, l, name="ffn_down")

    g_f = ln_f[None, :]
    return (_final_norm(x, g_f, n_first, nb, sp), _final_norm(x, g_f, 0, nb, ss))
```

```python
import functools
import itertools

import jax
import jax.numpy as jnp
from jax import lax
from jax.experimental import pallas as pl
from jax.experimental.pallas import tpu as pltpu

F32 = jnp.float32
BF16 = jnp.bfloat16

D_MODEL = 2048
M_HEADS = 4
M_HEAD_DIM = 256
M_WIDTH = 1024
G_HEADS = 4
G_VAL_DIM = 256
G_KEY_DIM = 128
G_VWIDTH = 1024
G_KWIDTH = 512
G_LOWRANK = 16
G_GATE_NORM = 16.0
D_FF = 5632
EPS = 1e-6
NEG = -1e30
LOG2E = 1.4426950408889634

_GROUPS = (("ga", D_MODEL), ("gb", D_MODEL), ("mq", M_WIDTH), ("mk", M_WIDTH), ("mv", M_WIDTH),
           ("mo", M_WIDTH), ("gq", G_KWIDTH), ("gk", G_KWIDTH), ("gv", G_VWIDTH), ("gg", G_VWIDTH))
COL_BLOCK = {}
BIG_WIDTH = 0
for _name, _w in _GROUPS:
    assert BIG_WIDTH % _w == 0
    COL_BLOCK[_name] = BIG_WIDTH // _w
    BIG_WIDTH += _w
SMALL_WIDTH = 128
GATE_ROWS = 4 * M_HEADS + 2 * G_LOWRANK
CHUNK = 128
SUBLANES = 8

VMEM_LIMIT = 56 * 1024 * 1024


def _cparams(sem):
    return pltpu.CompilerParams(dimension_semantics=sem, vmem_limit_bytes=VMEM_LIMIT)


def _log_sigmoid(x):
    return jnp.minimum(x, 0.0) - jnp.log1p(jnp.exp(-jnp.abs(x)))


def _sigmoid(x):
    return 1.0 / (1.0 + jnp.exp(-x))


def _rmsnorm(x, g):
    return x * lax.rsqrt(jnp.mean(x * x, axis=-1, keepdims=True) + EPS) * g


def _inproj_kernel(x_ref, g_ref, w_ref, b_ref, ws_ref, bs_ref, scan_ref,
                   z_ref, zs_ref, zst_ref, xn_ref, *, n_col):
    i = pl.program_id(0)
    j = pl.program_id(1)
    last = n_col - 1
    cur = i % 2

    def matmul():
        z_ref[...] = jnp.dot(xn_ref[cur], w_ref[...], preferred_element_type=F32) + b_ref[...]

    def normalise(slot):
        xn_ref[slot] = _rmsnorm(x_ref[...], g_ref[...]).astype(BF16)

    def small_outputs():
        xn = xn_ref[cur]
        zs = jnp.dot(xn, ws_ref[...], preferred_element_type=F32) + bs_ref[...]
        zs_ref[...] = zs
        zt = zs.T
        n_derived = 6 * SUBLANES
        zst_ref[0:GATE_ROWS, :] = zt[0:GATE_ROWS, :]
        zst_ref[GATE_ROWS + n_derived:, :] = zt[GATE_ROWS + n_derived:, :]
        nchunk = zst_ref.shape[1] // CHUNK
        chunks = [slice(c * CHUNK, (c + 1) * CHUNK) for c in range(nchunk)]
        i_pre = jnp.concatenate([zt[0:8, cols] for cols in chunks], axis=0)
        lf = _log_sigmoid(jnp.concatenate([zt[8:16, cols] for cols in chunks], axis=0))
        sub = lax.broadcasted_iota(jnp.int32, lf.shape, 0)
        fwd_row = (sub & (SUBLANES - 1)) < M_HEADS
        hi = lf.astype(BF16)
        r1 = lf - hi.astype(F32)
        mid = r1.astype(BF16)
        lo3 = (r1 - mid.astype(F32)).astype(BF16)
        sums = jnp.dot(jnp.concatenate([hi, mid, lo3], axis=1), scan_ref[...],
                       preferred_element_type=F32)
        a = jnp.where(fwd_row, sums[:, 0:CHUNK], sums[:, CHUNK:2 * CHUNK])
        a_tot = sums[:, 2 * CHUNK:3 * CHUNK]
        b = i_pre - a
        g = a_tot - a + i_pre
        cm = jnp.where(fwd_row, _lane_scan(b, jnp.maximum, -jnp.inf, False),
                       _lane_scan(b, jnp.maximum, -jnp.inf, True))
        g_max = jnp.broadcast_to(jnp.max(g, axis=1, keepdims=True), g.shape)
        for slot, val in enumerate((a, b, cm, g, a_tot, g_max)):
            lo = GATE_ROWS + slot * SUBLANES
            for c, cols in enumerate(chunks):
                zst_ref[lo:lo + SUBLANES, cols] = val[c * SUBLANES:(c + 1) * SUBLANES, :]

    @pl.when((i == 0) & (j == 0))
    def _():
        normalise(0)

    @pl.when(j == 1)
    def _():
        small_outputs()
        matmul()

    @pl.when(j == last)
    def _():
        normalise(1 - cur)
        matmul()

    @pl.when((j != 1) & (j != last))
    def _():
        matmul()


def _scan_matrix():
    s = jnp.arange(CHUNK)[:, None]
    t = jnp.arange(CHUNK)[None, :]
    one = jnp.concatenate([s <= t, s >= t, jnp.ones((CHUNK, CHUNK), bool)], axis=1).astype(BF16)
    return jnp.concatenate([one, one, one], axis=0)


def _inproj(x, g, w, layer, b, ws, bs, scan_mat, *, tm=1024, tn=1024):
    T = x.shape[0]
    N = w.shape[2]
    n_row, n_col = T // tm, N // tn
    assert n_col >= 3, "needs distinct first, second and last column steps"

    def x_block(i, j):
        return (jnp.minimum(i + (j == n_col - 1).astype(jnp.int32), n_row - 1), 0)

    return pl.pallas_call(
        functools.partial(_inproj_kernel, n_col=n_col),
        grid=(n_row, n_col),
        in_specs=[
            pl.BlockSpec((tm, D_MODEL), x_block),
            pl.BlockSpec((1, D_MODEL), lambda i, j: (0, 0)),
            pl.BlockSpec((None, D_MODEL, tn), lambda i, j: (layer, 0, j)),
            pl.BlockSpec((1, tn), lambda i, j: (0, j)),
            pl.BlockSpec((D_MODEL, SMALL_WIDTH), lambda i, j: (0, 0)),
            pl.BlockSpec((1, SMALL_WIDTH), lambda i, j: (0, 0)),
            pl.BlockSpec((3 * CHUNK, 3 * CHUNK), lambda i, j: (0, 0)),
        ],
        out_specs=[
            pl.BlockSpec((tm, tn), lambda i, j: (i, j)),
            pl.BlockSpec((tm, SMALL_WIDTH), lambda i, j: (i, 0)),
            pl.BlockSpec((SMALL_WIDTH, tm), lambda i, j: (0, i)),
        ],
        out_shape=[
            jax.ShapeDtypeStruct((T, N), F32),
            jax.ShapeDtypeStruct((T, SMALL_WIDTH), F32),
            jax.ShapeDtypeStruct((SMALL_WIDTH, T), F32),
        ],
        scratch_shapes=[pltpu.VMEM((2, tm, D_MODEL), BF16)],
        compiler_params=_cparams(("arbitrary", "arbitrary")),
        name="inproj",
    )(x, g, w, b, ws, bs, scan_mat)


def _head_norm_gate(hsum, gain, gate):
    y = hsum * lax.rsqrt(jnp.mean(hsum * hsum, axis=-1, keepdims=True) + EPS)
    return (y * gain * gate).astype(BF16)


def _mixout_kernel(x_ref, af_ref, ab_ref, bf_ref, bb_ref, mo_ref, gg_ref, ga_ref, gb_ref,
                   mgain_ref, ggain_ref, ln2_ref, wpa_ref, wpb_ref, wo_ref, out_ref, xn_ref,
                   ha_scr, hb_scr):
    for h in range(M_HEADS):
        sl = slice(h * M_HEAD_DIM, (h + 1) * M_HEAD_DIM)
        ha_scr[:, sl] = _head_norm_gate(af_ref[:, sl] + ab_ref[:, sl], mgain_ref[:, sl],
                                        _sigmoid(mo_ref[:, sl]))
    for h in range(G_HEADS):
        sl = slice(h * G_VAL_DIM, (h + 1) * G_VAL_DIM)
        gg = gg_ref[:, sl]
        hb_scr[:, sl] = _head_norm_gate(bf_ref[:, sl] + bb_ref[:, sl], ggain_ref[:, sl],
                                        gg * _sigmoid(gg))
    pa = jnp.dot(ha_scr[...], wpa_ref[...], preferred_element_type=F32)
    pb = jnp.dot(hb_scr[...], wpb_ref[...], preferred_element_type=F32)
    merged = (_sigmoid(ga_ref[...]) * pa + _sigmoid(gb_ref[...]) * pb).astype(BF16)
    x_new = x_ref[...] + jnp.dot(merged, wo_ref[...], preferred_element_type=F32)
    out_ref[...] = x_new
    xn_ref[...] = _rmsnorm(x_new, ln2_ref[...]).astype(BF16)


def _mixout(x, af, ab, bf, bb, z, mgain, ggain, ln2, wpa, wpb, wo, layer, *, tm=256):
    T = x.shape[0]
    mo_blk, gg_blk, ga_blk, gb_blk = (COL_BLOCK[n] for n in ("mo", "gg", "ga", "gb"))
    once = dict(pipeline_mode=pl.Buffered(1))
    return pl.pallas_call(
        _mixout_kernel,
        grid=(T // tm,),
        in_specs=[
            pl.BlockSpec((tm, D_MODEL), lambda i: (i, 0)),
            pl.BlockSpec((tm, M_WIDTH), lambda i: (i, 0)),
            pl.BlockSpec((tm, M_WIDTH), lambda i: (i, 0)),
            pl.BlockSpec((tm, G_VWIDTH), lambda i: (i, 0)),
            pl.BlockSpec((tm, G_VWIDTH), lambda i: (i, 0)),
            pl.BlockSpec((tm, M_WIDTH), lambda i: (i, mo_blk)),
            pl.BlockSpec((tm, G_VWIDTH), lambda i: (i, gg_blk)),
            pl.BlockSpec((tm, D_MODEL), lambda i: (i, ga_blk)),
            pl.BlockSpec((tm, D_MODEL), lambda i: (i, gb_blk)),
            pl.BlockSpec((1, M_WIDTH), lambda i: (0, 0)),
            pl.BlockSpec((1, G_VWIDTH), lambda i: (0, 0)),
            pl.BlockSpec((1, D_MODEL), lambda i: (0, 0)),
            pl.BlockSpec((None, M_WIDTH, D_MODEL), lambda i: (layer, 0, 0), **once),
            pl.BlockSpec((None, G_VWIDTH, D_MODEL), lambda i: (layer, 0, 0), **once),
            pl.BlockSpec((None, D_MODEL, D_MODEL), lambda i: (layer, 0, 0), **once),
        ],
        out_specs=[pl.BlockSpec((tm, D_MODEL), lambda i: (i, 0))] * 2,
        out_shape=[jax.ShapeDtypeStruct((T, D_MODEL), F32),
                   jax.ShapeDtypeStruct((T, D_MODEL), BF16)],
        scratch_shapes=[pltpu.VMEM((tm, M_WIDTH), BF16), pltpu.VMEM((tm, G_VWIDTH), BF16)],
        compiler_params=_cparams(("parallel",)),
        name="mixout",
    )(x, af, ab, bf, bb, z, z, z, z, mgain, ggain, ln2, wpa, wpb, wo)


def _resmm_kernel(x_ref, a_ref, w_ref, out_ref):
    out_ref[...] = x_ref[...] + jnp.dot(a_ref[...], w_ref[...], preferred_element_type=F32)


def _resmm(x, a, w, layer, *, tm=1024, tn=512, name="resmm"):
    T, K = a.shape
    N = w.shape[2]
    return pl.pallas_call(
        _resmm_kernel,
        grid=(T // tm, N // tn),
        in_specs=[
            pl.BlockSpec((tm, tn), lambda i, j: (i, j)),
            pl.BlockSpec((tm, K), lambda i, j: (i, 0)),
            pl.BlockSpec((None, K, tn), lambda i, j: (layer, 0, j)),
        ],
        out_specs=pl.BlockSpec((tm, tn), lambda i, j: (i, j)),
        out_shape=jax.ShapeDtypeStruct((T, N), F32),
        compiler_params=_cparams(("parallel", "arbitrary")),
        name=name,
    )(x, a, w)


def _ffn_up_kernel(xn_ref, wg_ref, wu_ref, h_ref):
    xn = xn_ref[...]
    gate = jnp.dot(xn, wg_ref[...], preferred_element_type=F32)
    up = jnp.dot(xn, wu_ref[...], preferred_element_type=F32)
    h_ref[...] = (gate * _sigmoid(gate) * up).astype(BF16)


def _ffn_up(xn, wgu, layer, *, tm=1024, tn=512):
    T = xn.shape[0]
    nj = D_FF // tn
    return pl.pallas_call(
        _ffn_up_kernel,
        grid=(T // tm, nj),
        in_specs=[
            pl.BlockSpec((tm, D_MODEL), lambda i, j: (i, 0)),
            pl.BlockSpec((None, D_MODEL, tn), lambda i, j: (layer, 0, j)),
            pl.BlockSpec((None, D_MODEL, tn), lambda i, j: (layer, 0, nj + j)),
        ],
        out_specs=pl.BlockSpec((tm, tn), lambda i, j: (i, j)),
        out_shape=jax.ShapeDtypeStruct((T, D_FF), BF16),
        compiler_params=_cparams(("parallel", "arbitrary")),
        name="ffn_up",
    )(xn, wgu, wgu)


def _final_norm_kernel(x_ref, g_ref, out_ref, *, nbatch):
    for bi in range(nbatch):
        out_ref[bi] = _rmsnorm(x_ref[bi * CHUNK:(bi + 1) * CHUNK, :], g_ref[...])


def _final_norm(x, g, blk_off, nbatch, seq):
    nblk = seq // CHUNK
    return pl.pallas_call(
        functools.partial(_final_norm_kernel, nbatch=nbatch),
        grid=(nblk,),
        in_specs=[
            pl.BlockSpec((nbatch * CHUNK, D_MODEL), lambda i: (blk_off + i, 0)),
            pl.BlockSpec((1, D_MODEL), lambda i: (0, 0)),
        ],
        out_specs=pl.BlockSpec((nbatch, CHUNK, D_MODEL), lambda i: (0, i, 0)),
        out_shape=jax.ShapeDtypeStruct((nbatch, seq, D_MODEL), F32),
        compiler_params=_cparams(("parallel",)),
        name="final_norm",
    )(x, g)


def _lane_scan(x, op, fill, reverse):
    n = x.shape[1]
    lane = lax.broadcasted_iota(jnp.int32, x.shape, 1)
    sh = 1
    while sh < n:
        if reverse:
            x = op(x, jnp.where(lane < n - sh, pltpu.roll(x, n - sh, axis=1), fill))
        else:
            x = op(x, jnp.where(lane >= sh, pltpu.roll(x, sh, axis=1), fill))
        sh *= 2
    return x


def _to_column(row, eye):
    return jnp.sum(jnp.where(eye, row, 0.0), axis=1, keepdims=True)


def _mlstm_step(qf_ref, kf_ref, vf_ref, gtf_ref, qb_ref, kb_ref, vb_ref, gtb_ref,
                hf_ref, hb_ref, c_scr, n_scr, m_scr, *, nbatch):
    L = CHUNK
    dh = M_HEAD_DIM

    rr = lax.broadcasted_iota(jnp.int32, (L, L), 0)
    cc = lax.broadcasted_iota(jnp.int32, (L, L), 1)
    eye = rr == cc

    dirs = ((qf_ref, kf_ref, vf_ref, gtf_ref, hf_ref), (qb_ref, kb_ref, vb_ref, gtb_ref, hb_ref))
    for d, (q_ref, k_ref, v_ref, gt_ref, out_ref) in enumerate(dirs):
        reverse = d == 1
        mask = (cc >= rr) if reverse else (cc <= rr)
        for bi in range(nbatch):
            rows = slice(bi * L, (bi + 1) * L)
            a, b, cm, g, a_tot, g_max = [
                gt_ref[GATE_ROWS + t * SUBLANES:GATE_ROWS + (t + 1) * SUBLANES, rows]
                for t in range(6)]
            m_old = m_scr[d * nbatch + bi]
            m_new = jnp.maximum(a_tot + m_old, g_max)
            big_m = jnp.maximum(m_old, cm)
            floor = jnp.exp(-(a + big_m))
            ksc = jnp.exp(g - m_new)
            decay = jnp.exp(a_tot + m_old - m_new)
            m_scr[d * nbatch + bi] = m_new

            for h in range(M_HEADS):
                r = d * M_HEADS + h
                ci = (d * nbatch + bi) * M_HEADS + h
                sl = slice(h * dh, (h + 1) * dh)
                m_c = _to_column(big_m[r:r + 1, :], eye)
                e_c = jnp.exp(m_old[r:r + 1, :] - m_c)
                f_c = _to_column(floor[r:r + 1, :], eye)
                k_c = _to_column(ksc[r:r + 1, :], eye)
                dec = jnp.concatenate([decay[r:r + 1, :]] * (dh // L), axis=1)

                q = q_ref[rows, sl] * (dh ** -0.5)
                k = k_ref[rows, sl]
                v_b = v_ref[rows, sl].astype(BF16)
                q_b = q.astype(BF16)
                c_old = c_scr[ci]
                n_old = n_scr[ci]

                qk = lax.dot_general(q_b, k.astype(BF16), (((1,), (1,)), ((), ())),
                                     preferred_element_type=F32)
                p = jnp.exp(jnp.where(mask, b[r:r + 1, :] - m_c, NEG))
                s = qk * p
                qn = q * n_old
                qn = sum(qn[:, t * L:(t + 1) * L] for t in range(dh // L))
                den = jnp.sum(s + e_c * qn, axis=1, keepdims=True)
                num = (jnp.dot(s.astype(BF16), v_b, preferred_element_type=F32)
                       + jnp.concatenate([e_c] * (dh // L), axis=1)
                       * jnp.dot(q_b, c_old.astype(BF16), preferred_element_type=F32))
                out_ref[rows, sl] = num / jnp.maximum(jnp.abs(den), f_c)

                kw = k * k_c
                c_scr[ci] = dec * c_old + lax.dot_general(
                    kw.astype(BF16), v_b, (((0,), (0,)), ((), ())), preferred_element_type=F32)
                n_scr[ci] = dec * n_old + jnp.sum(kw, axis=0, keepdims=True)
                yield


def _scan_blocks(n_first, n_total):
    def fwd(n):
        return n

    def bwd(n):
        return jnp.where(n < n_first, n_first - 1 - n, n_first + n_total - 1 - n)

    return fwd, bwd


def _mlstm_call_parts(z, zst, nbatch, n_first, n_total):
    T = z.shape[0]
    R = nbatch * CHUNK
    fwd, bwd = _scan_blocks(n_first, n_total)
    in_specs = []
    for blk in (fwd, bwd):
        in_specs += [
            pl.BlockSpec((R, M_WIDTH), lambda n, blk=blk: (blk(n), COL_BLOCK["mq"])),
            pl.BlockSpec((R, M_WIDTH), lambda n, blk=blk: (blk(n), COL_BLOCK["mk"])),
            pl.BlockSpec((R, M_WIDTH), lambda n, blk=blk: (blk(n), COL_BLOCK["mv"])),
            pl.BlockSpec((SMALL_WIDTH, R), lambda n, blk=blk: (0, blk(n))),
        ]
    nchain = 2 * nbatch * M_HEADS
    out_specs = [pl.BlockSpec((R, M_WIDTH), lambda n: (fwd(n), 0)),
                 pl.BlockSpec((R, M_WIDTH), lambda n: (bwd(n), 0))]
    out_shapes = [jax.ShapeDtypeStruct((T, M_WIDTH), F32)] * 2
    scratch = [pltpu.VMEM((nchain, M_HEAD_DIM, M_HEAD_DIM), F32),
               pltpu.VMEM((nchain, 1, M_HEAD_DIM), F32),
               pltpu.VMEM((2 * nbatch, 8, CHUNK), F32)]
    return in_specs, [z, z, z, zst, z, z, z, zst], out_specs, out_shapes, scratch


def _row_scan_sum(x, tri3):
    hi = x.astype(BF16)
    r1 = x - hi.astype(F32)
    mid = r1.astype(BF16)
    lo = (r1 - mid.astype(F32)).astype(BF16)
    return jnp.dot(tri3, jnp.concatenate([hi, mid, lo], axis=0), preferred_element_type=F32)


def _gla_attention(q, k, gcum, reverse, level_masks, eye):
    L, dk = q.shape
    nb = L // SUBLANES
    row = lax.broadcasted_iota(jnp.int32, (L, dk), 0)
    sub = lax.broadcasted_iota(jnp.int32, (nb, SUBLANES, dk), 1)
    g3 = gcum.reshape(nb, SUBLANES, dk)

    def ref_rows(half):
        first = half if reverse else half - 1
        if half >= SUBLANES:
            pieces = [jnp.broadcast_to(gcum[p * 2 * half + first:p * 2 * half + first + 1, :],
                                       (2 * half, dk)) for p in range(L // (2 * half))]
            return jnp.concatenate(pieces, axis=0)
        if half == 1:
            if reverse:
                ref3 = jnp.where((sub & 1) == 0, pltpu.roll(g3, SUBLANES - 1, axis=1), g3)
            else:
                ref3 = jnp.where((sub & 1) == 1, pltpu.roll(g3, 1, axis=1), g3)
            return ref3.reshape(L, dk)
        ref3 = None
        for p in range(SUBLANES // (2 * half)):
            r = p * 2 * half + first
            piece = jnp.broadcast_to(g3[:, r:r + 1, :], g3.shape)
            ref3 = piece if ref3 is None else jnp.where(sub >= p * 2 * half, piece, ref3)
        return ref3.reshape(L, dk)

    att = jnp.where(eye, jnp.sum(q * k, axis=1, keepdims=True), 0.0)
    for lb, lmask in level_masks:
        e = jnp.exp2(-jnp.abs(gcum - ref_rows(1 << lb)))
        q_side = ((row >> lb) & 1) == (0 if reverse else 1)
        x = (jnp.where(q_side, q, k) * e).astype(BF16)
        prod = lax.dot_general(x, x, (((1,), (1,)), ((), ())), preferred_element_type=F32)
        att = jnp.where(lmask, prod, att)
    return att


def _gla_masks(L, reverse):
    rr = lax.broadcasted_iota(jnp.int32, (L, L), 0)
    cc = lax.broadcasted_iota(jnp.int32, (L, L), 1)
    x = rr ^ cc
    causal = (rr < cc) if reverse else (rr > cc)
    levels = [(lb, causal & ((x >> lb) == 1)) for lb in range(L.bit_length() - 2, -1, -1)]
    eye = rr == cc
    tri = jnp.where(causal | eye, 1.0, 0.0).astype(BF16)
    return levels, eye, jnp.concatenate([tri, tri, tri], axis=1)


def _gla_step(qf_ref, kf_ref, vf_ref, zsf_ref, qb_ref, kb_ref, vb_ref, zsb_ref,
              wlrf_ref, wlrb_ref, blrf_ref, blrb_ref, of_ref, ob_ref, st_scr, *, nbatch):
    L = CHUNK
    dk = G_KEY_DIM
    dv = G_VAL_DIM

    dirs =((qf_ref, kf_ref, vf_ref, zsf_ref, wlrf_ref, blrf_ref, of_ref),
            (qb_ref, kb_ref, vb_ref, zsb_ref, wlrb_ref, blrb_ref, ob_ref))
    for d, (q_ref, k_ref, v_ref, zs_ref, wlr_ref, blr_ref, out_ref) in enumerate(dirs):
        reverse = d == 1
        level_masks, eye, tri3 = _gla_masks(L, reverse)
        for bi in range(nbatch):
            rows = slice(bi * L, (bi + 1) * L)
            pre = jnp.dot(zs_ref[rows, :].astype(BF16), wlr_ref[...],
                          preferred_element_type=F32) + blr_ref[...]
            g_all = _row_scan_sum(_log_sigmoid(pre) * (LOG2E / G_GATE_NORM), tri3)
            for h in range(G_HEADS):
                ci = (d * nbatch + bi) * G_HEADS + h
                ksl = slice(h * dk, (h + 1) * dk)
                vsl = slice(h * dv, (h + 1) * dv)
                gcum = g_all[:, ksl]
                q = q_ref[rows, ksl] * (dk ** -0.5)
                k = k_ref[rows, ksl]
                v_b = v_ref[rows, vsl].astype(BF16)
                st_old = st_scr[ci]

                att = _gla_attention(q, k, gcum, reverse, level_masks, eye)
                g_last = gcum[0:1, :] if reverse else gcum[L - 1:L, :]
                qg = (q * jnp.exp2(gcum)).astype(BF16)
                out_ref[rows, vsl] = (
                    jnp.dot(att.astype(BF16), v_b, preferred_element_type=F32)
                    + lax.dot_general(qg, st_old.astype(BF16), (((1,), (1,)), ((), ())),
                                      preferred_element_type=F32))
                kw = (k * jnp.exp2(g_last - gcum)).astype(BF16)
                st_scr[ci] = (jnp.exp2(g_last) * st_old
                              + lax.dot_general(v_b, kw, (((0,), (0,)), ((), ())),
                                                preferred_element_type=F32))
                yield


def _gla_call_parts(z, zs, wlr, blr, nbatch, n_first, n_total):
    T = z.shape[0]
    R = nbatch * CHUNK
    fwd, bwd = _scan_blocks(n_first, n_total)
    in_specs = []
    for blk in (fwd, bwd):
        in_specs += [
            pl.BlockSpec((R, G_KWIDTH), lambda n, blk=blk: (blk(n), COL_BLOCK["gq"])),
            pl.BlockSpec((R, G_KWIDTH), lambda n, blk=blk: (blk(n), COL_BLOCK["gk"])),
            pl.BlockSpec((R, G_VWIDTH), lambda n, blk=blk: (blk(n), COL_BLOCK["gv"])),
            pl.BlockSpec((R, SMALL_WIDTH), lambda n, blk=blk: (blk(n), 0)),
        ]
    in_specs += [pl.BlockSpec((SMALL_WIDTH, G_KWIDTH), lambda n: (0, 0))] * 2
    in_specs += [pl.BlockSpec((1, G_KWIDTH), lambda n: (0, 0))] * 2
    out_specs = [pl.BlockSpec((R, G_VWIDTH), lambda n: (fwd(n), 0)),
                 pl.BlockSpec((R, G_VWIDTH), lambda n: (bwd(n), 0))]
    out_shapes = [jax.ShapeDtypeStruct((T, G_VWIDTH), F32)] * 2
    scratch = [pltpu.VMEM((2 * nbatch * G_HEADS, G_VAL_DIM, G_KEY_DIM), F32)]
    operands = [z, z, z, zs, z, z, z, zs, wlr[0], wlr[1], blr[0], blr[1]]
    return in_specs, operands, out_specs, out_shapes, scratch


def _mixers_kernel(*refs, nbatch, n_reset, n_in, n_out):
    (m_in, g_in), (m_out, g_out) = n_in, n_out
    m_inputs, refs = refs[:m_in], refs[m_in:]
    g_inputs, refs = refs[:g_in], refs[g_in:]
    m_outputs, refs = refs[:m_out], refs[m_out:]
    g_outputs, refs = refs[:g_out], refs[g_out:]
    (c_scr, n_scr, m_scr), (st_scr,) = refs[:3], refs[3:]
    step = pl.program_id(0)

    @pl.when((step == 0) | (step == n_reset))
    def _():
        c_scr[...] = jnp.zeros(c_scr.shape, F32)
        n_scr[...] = jnp.zeros(n_scr.shape, F32)
        m_scr[...] = jnp.full(m_scr.shape, NEG, F32)
        st_scr[...] = jnp.zeros(st_scr.shape, F32)

    m_steps = _mlstm_step(*m_inputs, *m_outputs, c_scr, n_scr, m_scr, nbatch=nbatch)
    g_steps = _gla_step(*g_inputs, *g_outputs, st_scr, nbatch=nbatch)
    for _ in itertools.zip_longest(g_steps, m_steps):
        pass


def _mixers(z, zs, zst, wlr, blr, nbatch, n_first, n_total):
    m_parts = _mlstm_call_parts(z, zst, nbatch, n_first, n_total)
    g_parts = _gla_call_parts(z, zs, wlr, blr, nbatch, n_first, n_total)
    in_specs, operands, out_specs, out_shapes, scratch = (a + b for a, b in zip(m_parts, g_parts))
    return pl.pallas_call(
        functools.partial(_mixers_kernel, nbatch=nbatch, n_reset=n_first,
                          n_in=(len(m_parts[0]), len(g_parts[0])),
                          n_out=(len(m_parts[2]), len(g_parts[2]))),
        grid=(n_total,),
        in_specs=in_specs,
        out_specs=out_specs,
        out_shape=out_shapes,
        scratch_shapes=scratch,
        compiler_params=_cparams(("arbitrary",)),
        name="mixers",
    )(*operands)


def _split_cols(t):
    o_mi = 4 * M_WIDTH
    o_gq = o_mi + 4 * M_HEADS
    o_lr = o_gq + 2 * G_KWIDTH + 2 * G_VWIDTH
    o_ga = o_lr + 2 * G_LOWRANK
    big = jnp.concatenate([t[..., o_ga:], t[..., :o_mi], t[..., o_gq:o_lr]], axis=-1)
    small = jnp.concatenate([t[..., o_mi:o_gq], t[..., o_lr:o_ga]], axis=-1)
    pad = [(0, 0)] * (t.ndim - 1) + [(0, SMALL_WIDTH - GATE_ROWS)]
    return big, jnp.pad(small, pad)


def _low_rank_weights(w_lr2):
    out = []
    for d in range(2):
        lo = 4 * M_HEADS + d * G_LOWRANK
        out.append(jnp.zeros((SMALL_WIDTH, G_KWIDTH), F32).at[lo:lo + G_LOWRANK].set(w_lr2[d])
                   .astype(BF16))
    return out


def _interleave_kernel(xs_ref, xp_ref, out_ref, *, nbatch, n_first):
    step = pl.program_id(0)

    @pl.when(step < n_first)
    def _():
        for bi in range(nbatch):
            out_ref[bi * CHUNK:(bi + 1) * CHUNK, :] = xs_ref[bi, 0]

    @pl.when(step >= n_first)
    def _():
        for bi in range(nbatch):
            out_ref[bi * CHUNK:(bi + 1) * CHUNK, :] = xp_ref[bi, 0]


def _interleave(x_first, x_second):
    nb, s1, D = x_first.shape
    s2 = x_second.shape[1]
    n_first = s1 // CHUNK
    n_second = s2 // CHUNK
    blk = (nb, 1, CHUNK, D)
    return pl.pallas_call(
        functools.partial(_interleave_kernel, nbatch=nb, n_first=n_first),
        grid=(n_first + n_second,),
        in_specs=[
            pl.BlockSpec(blk, lambda n: (0, jnp.minimum(n, n_first - 1), 0, 0)),
            pl.BlockSpec(blk, lambda n: (0, jnp.maximum(n - n_first, 0), 0, 0)),
        ],
        out_specs=pl.BlockSpec((nb * CHUNK, D), lambda n: (n, 0)),
        out_shape=jax.ShapeDtypeStruct((nb * (s1 + s2), D), x_first.dtype),
        compiler_params=_cparams(("arbitrary",)),
        name="interleave",
    )(x_first.reshape(nb, n_first, CHUNK, D), x_second.reshape(nb, n_second, CHUNK, D))


def kernel(x_prompt, x_sample, ln1, w_in, b_in, m_norm, w_lr2, b_lr2, g_norm, w_pa, w_pb, w_o,
           ln2, w_gu, w_down, ln_f):
    bp, sp, _ = x_prompt.shape
    bs, ss, _ = x_sample.shape
    assert bp == bs, "both request groups must have the same batch size"
    nb = bs
    n_first = ss // CHUNK
    n_total = n_first + sp // CHUNK
    x = _interleave(x_sample, x_prompt)
    scan_mat = _scan_matrix()

    w_big, w_small = _split_cols(w_in.astype(BF16))
    b_big, b_small = _split_cols(b_in)
    w_pa, w_pb, w_o, w_gu, w_down = (t.astype(BF16) for t in (w_pa, w_pb, w_o, w_gu, w_down))

    for l in range(ln1.shape[0]):
        wlr = _low_rank_weights(w_lr2[l])
        blr = [b_lr2[l, 0][None, :], b_lr2[l, 1][None, :]]
        z, zs, zst = _inproj(x, ln1[l][None, :], w_big, l, b_big[l][None, :], w_small[l],
                             b_small[l][None, :], scan_mat)
        af, ab, bf, bb = _mixers(z, zs, zst, wlr, blr, nb, n_first, n_total)
        x, xn = _mixout(x, af, ab, bf, bb, z, m_norm[l][None, :], g_norm[l][None, :],
                        ln2[l][None, :], w_pa, w_pb, w_o, l)
        hmid = _ffn_up(xn, w_gu, l)
        x = _resmm(x, hmid, w_down, l, name="ffn_down")

    g_f = ln_f[None, :]
    return (_final_norm(x, g_f, n_first, nb, sp), _final_norm(x, g_f, 0, nb, ss))
```

```python
import functools
import itertools

import jax
import jax.numpy as jnp
from jax import lax
from jax.experimental import pallas as pl
from jax.experimental.pallas import tpu as pltpu

F32 = jnp.float32
BF16 = jnp.bfloat16

D_MODEL = 2048
M_HEADS = 4
M_HEAD_DIM = 256
M_WIDTH = 1024
G_HEADS = 4
G_VAL_DIM = 256
G_KEY_DIM = 128
G_VWIDTH = 1024
G_KWIDTH = 512
G_LOWRANK = 16
G_GATE_NORM = 16.0
D_FF = 5632
EPS = 1e-6
NEG = -1e30
LOG2E = 1.4426950408889634

_GROUPS = (("ga", D_MODEL), ("gb", D_MODEL), ("mq", M_WIDTH), ("mk", M_WIDTH), ("mv", M_WIDTH),
           ("mo", M_WIDTH), ("gq", G_KWIDTH), ("gk", G_KWIDTH), ("gv", G_VWIDTH), ("gg", G_VWIDTH))
COL_BLOCK = {}
BIG_WIDTH = 0
for _name, _w in _GROUPS:
    assert BIG_WIDTH % _w == 0
    COL_BLOCK[_name] = BIG_WIDTH // _w
    BIG_WIDTH += _w
SMALL_WIDTH = 128
GATE_ROWS = 4 * M_HEADS + 2 * G_LOWRANK
CHUNK = 128
SUBLANES = 8

VMEM_LIMIT = 56 * 1024 * 1024


def _cparams(sem):
    return pltpu.CompilerParams(dimension_semantics=sem, vmem_limit_bytes=VMEM_LIMIT)


def _log_sigmoid(x):
    return jnp.minimum(x, 0.0) - jnp.log1p(jnp.exp(-jnp.abs(x)))


def _sigmoid(x):
    return 1.0 / (1.0 + jnp.exp(-x))


def _rmsnorm(x, g):
    return x * lax.rsqrt(jnp.mean(x * x, axis=-1, keepdims=True) + EPS) * g


def _inproj_kernel(x_ref, g_ref, w_ref, b_ref, ws_ref, bs_ref, scan_ref,
                   z_ref, zs_ref, zst_ref, xn_ref, *, n_col):
    i = pl.program_id(0)
    j = pl.program_id(1)
    last = n_col - 1
    cur = i % 2

    def matmul():
        z_ref[...] = jnp.dot(xn_ref[cur], w_ref[...], preferred_element_type=F32) + b_ref[...]

    def normalise(slot):
        xn_ref[slot] = _rmsnorm(x_ref[...], g_ref[...]).astype(BF16)

    def small_outputs():
        xn = xn_ref[cur]
        zs = jnp.dot(xn, ws_ref[...], preferred_element_type=F32) + bs_ref[...]
        zs_ref[...] = zs
        zt = zs.T
        n_derived = 6 * SUBLANES
        zst_ref[0:GATE_ROWS, :] = zt[0:GATE_ROWS, :]
        zst_ref[GATE_ROWS + n_derived:, :] = zt[GATE_ROWS + n_derived:, :]
        nchunk = zst_ref.shape[1] // CHUNK
        chunks = [slice(c * CHUNK, (c + 1) * CHUNK) for c in range(nchunk)]
        i_pre = jnp.concatenate([zt[0:8, cols] for cols in chunks], axis=0)
        lf = _log_sigmoid(jnp.concatenate([zt[8:16, cols] for cols in chunks], axis=0))
        sub = lax.broadcasted_iota(jnp.int32, lf.shape, 0)
        fwd_row = (sub & (SUBLANES - 1)) < M_HEADS
        hi = lf.astype(BF16)
        r1 = lf - hi.astype(F32)
        mid = r1.astype(BF16)
        lo3 = (r1 - mid.astype(F32)).astype(BF16)
        sums = jnp.dot(jnp.concatenate([hi, mid, lo3], axis=1), scan_ref[...],
                       preferred_element_type=F32)
        a = jnp.where(fwd_row, sums[:, 0:CHUNK], sums[:, CHUNK:2 * CHUNK])
        a_tot = sums[:, 2 * CHUNK:3 * CHUNK]
        b = i_pre - a
        g = a_tot - a + i_pre
        cm = jnp.where(fwd_row, _lane_scan(b, jnp.maximum, -jnp.inf, False),
                       _lane_scan(b, jnp.maximum, -jnp.inf, True))
        g_max = jnp.broadcast_to(jnp.max(g, axis=1, keepdims=True), g.shape)
        for slot, val in enumerate((a, b, cm, g, a_tot, g_max)):
            lo = GATE_ROWS + slot * SUBLANES
            for c, cols in enumerate(chunks):
                zst_ref[lo:lo + SUBLANES, cols] = val[c * SUBLANES:(c + 1) * SUBLANES, :]

    @pl.when((i == 0) & (j == 0))
    def _():
        normalise(0)

    @pl.when(j == 1)
    def _():
        small_outputs()
        matmul()

    @pl.when(j == last)
    def _():
        normalise(1 - cur)
        matmul()

    @pl.when((j != 1) & (j != last))
    def _():
        matmul()


def _scan_matrix():
    s = jnp.arange(CHUNK)[:, None]
    t = jnp.arange(CHUNK)[None, :]
    one = jnp.concatenate([s <= t, s >= t, jnp.ones((CHUNK, CHUNK), bool)], axis=1).astype(BF16)
    return jnp.concatenate([one, one, one], axis=0)


def _inproj(x, g, w, layer, b, ws, bs, scan_mat, *, tm=1024, tn=1024):
    T = x.shape[0]
    N = w.shape[2]
    n_row, n_col = T // tm, N // tn
    assert n_col >= 3, "needs distinct first, second and last column steps"

    def x_block(i, j):
        return (jnp.minimum(i + (j == n_col - 1).astype(jnp.int32), n_row - 1), 0)

    return pl.pallas_call(
        functools.partial(_inproj_kernel, n_col=n_col),
        grid=(n_row, n_col),
        in_specs=[
            pl.BlockSpec((tm, D_MODEL), x_block),
            pl.BlockSpec((1, D_MODEL), lambda i, j: (0, 0)),
            pl.BlockSpec((None, D_MODEL, tn), lambda i, j: (layer, 0, j)),
            pl.BlockSpec((1, tn), lambda i, j: (0, j)),
            pl.BlockSpec((D_MODEL, SMALL_WIDTH), lambda i, j: (0, 0)),
            pl.BlockSpec((1, SMALL_WIDTH), lambda i, j: (0, 0)),
            pl.BlockSpec((3 * CHUNK, 3 * CHUNK), lambda i, j: (0, 0)),
        ],
        out_specs=[
            pl.BlockSpec((tm, tn), lambda i, j: (i, j)),
            pl.BlockSpec((tm, SMALL_WIDTH), lambda i, j: (i, 0)),
            pl.BlockSpec((SMALL_WIDTH, tm), lambda i, j: (0, i)),
        ],
        out_shape=[
            jax.ShapeDtypeStruct((T, N), F32),
            jax.ShapeDtypeStruct((T, SMALL_WIDTH), F32),
            jax.ShapeDtypeStruct((SMALL_WIDTH, T), F32),
        ],
        scratch_shapes=[pltpu.VMEM((2, tm, D_MODEL), BF16)],
        compiler_params=_cparams(("arbitrary", "arbitrary")),
        name="inproj",
    )(x, g, w, b, ws, bs, scan_mat)


def _head_norm_gate(hsum, gain, gate):
    y = hsum * lax.rsqrt(jnp.mean(hsum * hsum, axis=-1, keepdims=True) + EPS)
    return (y * gain * gate).astype(BF16)


def _mixout_kernel(x_ref, af_ref, ab_ref, bf_ref, bb_ref, mo_ref, gg_ref, ga_ref, gb_ref,
                   mgain_ref, ggain_ref, ln2_ref, wpa_ref, wpb_ref, wo_ref, out_ref, xn_ref,
                   ha_scr, hb_scr):
    for h in range(M_HEADS):
        sl = slice(h * M_HEAD_DIM, (h + 1) * M_HEAD_DIM)
        ha_scr[:, sl] = _head_norm_gate(af_ref[:, sl] + ab_ref[:, sl], mgain_ref[:, sl],
                                        _sigmoid(mo_ref[:, sl]))
    for h in range(G_HEADS):
        sl = slice(h * G_VAL_DIM, (h + 1) * G_VAL_DIM)
        gg = gg_ref[:, sl]
        hb_scr[:, sl] = _head_norm_gate(bf_ref[:, sl] + bb_ref[:, sl], ggain_ref[:, sl],
                                        gg * _sigmoid(gg))
    pa = jnp.dot(ha_scr[...], wpa_ref[...], preferred_element_type=F32)
    pb = jnp.dot(hb_scr[...], wpb_ref[...], preferred_element_type=F32)
    merged = (_sigmoid(ga_ref[...]) * pa + _sigmoid(gb_ref[...]) * pb).astype(BF16)
    x_new = x_ref[...] + jnp.dot(merged, wo_ref[...], preferred_element_type=F32)
    out_ref[...] = x_new
    xn_ref[...] = _rmsnorm(x_new, ln2_ref[...]).astype(BF16)


def _mixout(x, af, ab, bf, bb, z, mgain, ggain, ln2, wpa, wpb, wo, layer, *, tm=256):
    T = x.shape[0]
    mo_blk, gg_blk, ga_blk, gb_blk = (COL_BLOCK[n] for n in ("mo", "gg", "ga", "gb"))
    once = dict(pipeline_mode=pl.Buffered(1))
    return pl.pallas_call(
        _mixout_kernel,
        grid=(T // tm,),
        in_specs=[
            pl.BlockSpec((tm, D_MODEL), lambda i: (i, 0)),
            pl.BlockSpec((tm, M_WIDTH), lambda i: (i, 0)),
            pl.BlockSpec((tm, M_WIDTH), lambda i: (i, 0)),
            pl.BlockSpec((tm, G_VWIDTH), lambda i: (i, 0)),
            pl.BlockSpec((tm, G_VWIDTH), lambda i: (i, 0)),
            pl.BlockSpec((tm, M_WIDTH), lambda i: (i, mo_blk)),
            pl.BlockSpec((tm, G_VWIDTH), lambda i: (i, gg_blk)),
            pl.BlockSpec((tm, D_MODEL), lambda i: (i, ga_blk)),
            pl.BlockSpec((tm, D_MODEL), lambda i: (i, gb_blk)),
            pl.BlockSpec((1, M_WIDTH), lambda i: (0, 0)),
            pl.BlockSpec((1, G_VWIDTH), lambda i: (0, 0)),
            pl.BlockSpec((1, D_MODEL), lambda i: (0, 0)),
            pl.BlockSpec((None, M_WIDTH, D_MODEL), lambda i: (layer, 0, 0), **once),
            pl.BlockSpec((None, G_VWIDTH, D_MODEL), lambda i: (layer, 0, 0), **once),
            pl.BlockSpec((None, D_MODEL, D_MODEL), lambda i: (layer, 0, 0), **once),
        ],
        out_specs=[pl.BlockSpec((tm, D_MODEL), lambda i: (i, 0))] * 2,
        out_shape=[jax.ShapeDtypeStruct((T, D_MODEL), F32),
                   jax.ShapeDtypeStruct((T, D_MODEL), BF16)],
        scratch_shapes=[pltpu.VMEM((tm, M_WIDTH), BF16), pltpu.VMEM((tm, G_VWIDTH), BF16)],
        compiler_params=_cparams(("parallel",)),
        name="mixout",
    )(x, af, ab, bf, bb, z, z, z, z, mgain, ggain, ln2, wpa, wpb, wo)


def _resmm_kernel(x_ref, a_ref, w_ref, out_ref):
    out_ref[...] = x_ref[...] + jnp.dot(a_ref[...], w_ref[...], preferred_element_type=F32)


def _resmm(x, a, w, layer, *, tm=1024, tn=512, name="resmm"):
    T, K = a.shape
    N = w.shape[2]
    return pl.pallas_call(
        _resmm_kernel,
        grid=(T // tm, N // tn),
        in_specs=[
            pl.BlockSpec((tm, tn), lambda i, j: (i, j)),
            pl.BlockSpec((tm, K), lambda i, j: (i, 0)),
            pl.BlockSpec((None, K, tn), lambda i, j: (layer, 0, j)),
        ],
        out_specs=pl.BlockSpec((tm, tn), lambda i, j: (i, j)),
        out_shape=jax.ShapeDtypeStruct((T, N), F32),
        compiler_params=_cparams(("parallel", "arbitrary")),
        name=name,
    )(x, a, w)


def _ffn_up_kernel(xn_ref, wg_ref, wu_ref, h_ref):
    xn = xn_ref[...]
    gate = jnp.dot(xn, wg_ref[...], preferred_element_type=F32)
    up = jnp.dot(xn, wu_ref[...], preferred_element_type=F32)
    h_ref[...] = (gate * _sigmoid(gate) * up).astype(BF16)


def _ffn_up(xn, wgu, layer, *, tm=1024, tn=512):
    T = xn.shape[0]
    nj = D_FF // tn
    return pl.pallas_call(
        _ffn_up_kernel,
        grid=(T // tm, nj),
        in_specs=[
            pl.BlockSpec((tm, D_MODEL), lambda i, j: (i, 0)),
            pl.BlockSpec((None, D_MODEL, tn), lambda i, j: (layer, 0, j)),
            pl.BlockSpec((None, D_MODEL, tn), lambda i, j: (layer, 0, nj + j)),
        ],
        out_specs=pl.BlockSpec((tm, tn), lambda i, j: (i, j)),
        out_shape=jax.ShapeDtypeStruct((T, D_FF), BF16),
        compiler_params=_cparams(("parallel", "arbitrary")),
        name="ffn_up",
    )(xn, wgu, wgu)


def _final_norm_kernel(x_ref, g_ref, out_ref, *, nbatch):
    for bi in range(nbatch):
        out_ref[bi] = _rmsnorm(x_ref[bi * CHUNK:(bi + 1) * CHUNK, :], g_ref[...])


def _final_norm(x, g, blk_off, nbatch, seq):
    nblk = seq // CHUNK
    return pl.pallas_call(
        functools.partial(_final_norm_kernel, nbatch=nbatch),
        grid=(nblk,),
        in_specs=[
            pl.BlockSpec((nbatch * CHUNK, D_MODEL), lambda i: (blk_off + i, 0)),
            pl.BlockSpec((1, D_MODEL), lambda i: (0, 0)),
        ],
        out_specs=pl.BlockSpec((nbatch, CHUNK, D_MODEL), lambda i: (0, i, 0)),
        out_shape=jax.ShapeDtypeStruct((nbatch, seq, D_MODEL), F32),
        compiler_params=_cparams(("parallel",)),
        name="final_norm",
    )(x, g)


def _lane_scan(x, op, fill, reverse):
    n = x.shape[1]
    lane = lax.broadcasted_iota(jnp.int32, x.shape, 1)
    sh = 1
    while sh < n:
        if reverse:
            x = op(x, jnp.where(lane < n - sh, pltpu.roll(x, n - sh, axis=1), fill))
        else:
            x = op(x, jnp.where(lane >= sh, pltpu.roll(x, sh, axis=1), fill))
        sh *= 2
    return x


def _to_column(row, eye):
    return jnp.sum(jnp.where(eye, row, 0.0), axis=1, keepdims=True)


def _mlstm_step(qf_ref, kf_ref, vf_ref, gtf_ref, qb_ref, kb_ref, vb_ref, gtb_ref,
                hf_ref, hb_ref, c_scr, n_scr, m_scr, *, nbatch):
    L = CHUNK
    dh = M_HEAD_DIM

    rr = lax.broadcasted_iota(jnp.int32, (L, L), 0)
    cc = lax.broadcasted_iota(jnp.int32, (L, L), 1)
    eye = rr == cc

    dirs = ((qf_ref, kf_ref, vf_ref, gtf_ref, hf_ref), (qb_ref, kb_ref, vb_ref, gtb_ref, hb_ref))
    for d, (q_ref, k_ref, v_ref, gt_ref, out_ref) in enumerate(dirs):
        reverse = d == 1
        mask = (cc >= rr) if reverse else (cc <= rr)
        for bi in range(nbatch):
            rows = slice(bi * L, (bi + 1) * L)
            a, b, cm, g, a_tot, g_max = [
                gt_ref[GATE_ROWS + t * SUBLANES:GATE_ROWS + (t + 1) * SUBLANES, rows]
                for t in range(6)]
            m_old = m_scr[d * nbatch + bi]
            m_new = jnp.maximum(a_tot + m_old, g_max)
            big_m = jnp.maximum(m_old, cm)
            floor = jnp.exp(-(a + big_m))
            ksc = jnp.exp(g - m_new)
            decay = jnp.exp(a_tot + m_old - m_new)
            m_scr[d * nbatch + bi] = m_new

            for h in range(M_HEADS):
                r = d * M_HEADS + h
                ci = (d * nbatch + bi) * M_HEADS + h
                sl = slice(h * dh, (h + 1) * dh)
                m_c = _to_column(big_m[r:r + 1, :], eye)
                e_c = jnp.exp(m_old[r:r + 1, :] - m_c)
                f_c = _to_column(floor[r:r + 1, :], eye)
                k_c = _to_column(ksc[r:r + 1, :], eye)
                dec = jnp.concatenate([decay[r:r + 1, :]] * (dh // L), axis=1)

                q = q_ref[rows, sl] * (dh ** -0.5)
                k = k_ref[rows, sl]
                v_b = v_ref[rows, sl].astype(BF16)
                q_b = q.astype(BF16)
                c_old = c_scr[ci]
                n_old = n_scr[ci]

                qk = lax.dot_general(q_b, k.astype(BF16), (((1,), (1,)), ((), ())),
                                     preferred_element_type=F32)
                p = jnp.exp(jnp.where(mask, b[r:r + 1, :] - m_c, NEG))
                s = qk * p
                qn = q * n_old
                qn = sum(qn[:, t * L:(t + 1) * L] for t in range(dh // L))
                den = jnp.sum(s + e_c * qn, axis=1, keepdims=True)
                num = (jnp.dot(s.astype(BF16), v_b, preferred_element_type=F32)
                       + jnp.concatenate([e_c] * (dh // L), axis=1)
                       * jnp.dot(q_b, c_old.astype(BF16), preferred_element_type=F32))
                out_ref[rows, sl] = num / jnp.maximum(jnp.abs(den), f_c)
                yield

                kw = k * k_c
                c_scr[ci] = dec * c_old + lax.dot_general(
                    kw.astype(BF16), v_b, (((0,), (0,)), ((), ())), preferred_element_type=F32)
                n_scr[ci] = dec * n_old + jnp.sum(kw, axis=0, keepdims=True)
                yield


def _scan_blocks(n_first, n_total):
    def fwd(n):
        return n

    def bwd(n):
        return jnp.where(n < n_first, n_first - 1 - n, n_first + n_total - 1 - n)

    return fwd, bwd


def _mlstm_call_parts(z, zst, nbatch, n_first, n_total):
    T = z.shape[0]
    R = nbatch * CHUNK
    fwd, bwd = _scan_blocks(n_first, n_total)
    in_specs = []
    for blk in (fwd, bwd):
        in_specs += [
            pl.BlockSpec((R, M_WIDTH), lambda n, blk=blk: (blk(n), COL_BLOCK["mq"])),
            pl.BlockSpec((R, M_WIDTH), lambda n, blk=blk: (blk(n), COL_BLOCK["mk"])),
            pl.BlockSpec((R, M_WIDTH), lambda n, blk=blk: (blk(n), COL_BLOCK["mv"])),
            pl.BlockSpec((SMALL_WIDTH, R), lambda n, blk=blk: (0, blk(n))),
        ]
    nchain = 2 * nbatch * M_HEADS
    out_specs = [pl.BlockSpec((R, M_WIDTH), lambda n: (fwd(n), 0)),
                 pl.BlockSpec((R, M_WIDTH), lambda n: (bwd(n), 0))]
    out_shapes = [jax.ShapeDtypeStruct((T, M_WIDTH), F32)] * 2
    scratch = [pltpu.VMEM((nchain, M_HEAD_DIM, M_HEAD_DIM), F32),
               pltpu.VMEM((nchain, 1, M_HEAD_DIM), F32),
               pltpu.VMEM((2 * nbatch, 8, CHUNK), F32)]
    return in_specs, [z, z, z, zst, z, z, z, zst], out_specs, out_shapes, scratch


def _row_scan_sum(x, tri3):
    hi = x.astype(BF16)
    r1 = x - hi.astype(F32)
    mid = r1.astype(BF16)
    lo = (r1 - mid.astype(F32)).astype(BF16)
    return jnp.dot(tri3, jnp.concatenate([hi, mid, lo], axis=0), preferred_element_type=F32)


def _gla_attention(q, k, gcum, reverse, level_masks, eye):
    L, dk = q.shape
    nb = L // SUBLANES
    row = lax.broadcasted_iota(jnp.int32, (L, dk), 0)
    sub = lax.broadcasted_iota(jnp.int32, (nb, SUBLANES, dk), 1)
    g3 = gcum.reshape(nb, SUBLANES, dk)

    def ref_rows(half):
        first = half if reverse else half - 1
        if half >= SUBLANES:
            pieces = [jnp.broadcast_to(gcum[p * 2 * half + first:p * 2 * half + first + 1, :],
                                       (2 * half, dk)) for p in range(L // (2 * half))]
            return jnp.concatenate(pieces, axis=0)
        if half == 1:
            if reverse:
                ref3 = jnp.where((sub & 1) == 0, pltpu.roll(g3, SUBLANES - 1, axis=1), g3)
            else:
                ref3 = jnp.where((sub & 1) == 1, pltpu.roll(g3, 1, axis=1), g3)
            return ref3.reshape(L, dk)
        ref3 = None
        for p in range(SUBLANES // (2 * half)):
            r = p * 2 * half + first
            piece = jnp.broadcast_to(g3[:, r:r + 1, :], g3.shape)
            ref3 = piece if ref3 is None else jnp.where(sub >= p * 2 * half, piece, ref3)
        return ref3.reshape(L, dk)

    att = jnp.where(eye, jnp.sum(q * k, axis=1, keepdims=True), 0.0)
    for lb, lmask in level_masks:
        e = jnp.exp2(-jnp.abs(gcum - ref_rows(1 << lb)))
        q_side = ((row >> lb) & 1) == (0 if reverse else 1)
        x = (jnp.where(q_side, q, k) * e).astype(BF16)
        prod = lax.dot_general(x, x, (((1,), (1,)), ((), ())), preferred_element_type=F32)
        att = jnp.where(lmask, prod, att)
    return att


def _gla_masks(L, reverse):
    rr = lax.broadcasted_iota(jnp.int32, (L, L), 0)
    cc = lax.broadcasted_iota(jnp.int32, (L, L), 1)
    x = rr ^ cc
    causal = (rr < cc) if reverse else (rr > cc)
    levels = [(lb, causal & ((x >> lb) == 1)) for lb in range(L.bit_length() - 2, -1, -1)]
    eye = rr == cc
    tri = jnp.where(causal | eye, 1.0, 0.0).astype(BF16)
    return levels, eye, jnp.concatenate([tri, tri, tri], axis=1)


def _gla_step(qf_ref, kf_ref, vf_ref, zsf_ref, qb_ref, kb_ref, vb_ref, zsb_ref,
              wlrf_ref, wlrb_ref, blrf_ref, blrb_ref, of_ref, ob_ref, st_scr, *, nbatch):
    L = CHUNK
    dk = G_KEY_DIM
    dv = G_VAL_DIM

    dirs =((qf_ref, kf_ref, vf_ref, zsf_ref, wlrf_ref, blrf_ref, of_ref),
            (qb_ref, kb_ref, vb_ref, zsb_ref, wlrb_ref, blrb_ref, ob_ref))
    for d, (q_ref, k_ref, v_ref, zs_ref, wlr_ref, blr_ref, out_ref) in enumerate(dirs):
        reverse = d == 1
        level_masks, eye, tri3 = _gla_masks(L, reverse)
        for bi in range(nbatch):
            rows = slice(bi * L, (bi + 1) * L)
            pre = jnp.dot(zs_ref[rows, :].astype(BF16), wlr_ref[...],
                          preferred_element_type=F32) + blr_ref[...]
            g_all = _row_scan_sum(_log_sigmoid(pre) * (LOG2E / G_GATE_NORM), tri3)
            for h in range(G_HEADS):
                ci = (d * nbatch + bi) * G_HEADS + h
                ksl = slice(h * dk, (h + 1) * dk)
                vsl = slice(h * dv, (h + 1) * dv)
                gcum = g_all[:, ksl]
                q = q_ref[rows, ksl] * (dk ** -0.5)
                k = k_ref[rows, ksl]
                v_b = v_ref[rows, vsl].astype(BF16)
                st_old = st_scr[ci]

                att = _gla_attention(q, k, gcum, reverse, level_masks, eye)
                yield
                g_last = gcum[0:1, :] if reverse else gcum[L - 1:L, :]
                qg = (q * jnp.exp2(gcum)).astype(BF16)
                out_ref[rows, vsl] = (
                    jnp.dot(att.astype(BF16), v_b, preferred_element_type=F32)
                    + lax.dot_general(qg, st_old.astype(BF16), (((1,), (1,)), ((), ())),
                                      preferred_element_type=F32))
                kw = (k * jnp.exp2(g_last - gcum)).astype(BF16)
                st_scr[ci] = (jnp.exp2(g_last) * st_old
                              + lax.dot_general(v_b, kw, (((0,), (0,)), ((), ())),
                                                preferred_element_type=F32))
                yield


def _gla_call_parts(z, zs, wlr, blr, nbatch, n_first, n_total):
    T = z.shape[0]
    R = nbatch * CHUNK
    fwd, bwd = _scan_blocks(n_first, n_total)
    in_specs = []
    for blk in (fwd, bwd):
        in_specs += [
            pl.BlockSpec((R, G_KWIDTH), lambda n, blk=blk: (blk(n), COL_BLOCK["gq"])),
            pl.BlockSpec((R, G_KWIDTH), lambda n, blk=blk: (blk(n), COL_BLOCK["gk"])),
            pl.BlockSpec((R, G_VWIDTH), lambda n, blk=blk: (blk(n), COL_BLOCK["gv"])),
            pl.BlockSpec((R, SMALL_WIDTH), lambda n, blk=blk: (blk(n), 0)),
        ]
    in_specs += [pl.BlockSpec((SMALL_WIDTH, G_KWIDTH), lambda n: (0, 0))] * 2
    in_specs += [pl.BlockSpec((1, G_KWIDTH), lambda n: (0, 0))] * 2
    out_specs = [pl.BlockSpec((R, G_VWIDTH), lambda n: (fwd(n), 0)),
                 pl.BlockSpec((R, G_VWIDTH), lambda n: (bwd(n), 0))]
    out_shapes = [jax.ShapeDtypeStruct((T, G_VWIDTH), F32)] * 2
    scratch = [pltpu.VMEM((2 * nbatch * G_HEADS, G_VAL_DIM, G_KEY_DIM), F32)]
    operands = [z, z, z, zs, z, z, z, zs, wlr[0], wlr[1], blr[0], blr[1]]
    return in_specs, operands, out_specs, out_shapes, scratch


def _mixers_kernel(*refs, nbatch, n_reset, n_in, n_out):
    (m_in, g_in), (m_out, g_out) = n_in, n_out
    m_inputs, refs = refs[:m_in], refs[m_in:]
    g_inputs, refs = refs[:g_in], refs[g_in:]
    m_outputs, refs = refs[:m_out], refs[m_out:]
    g_outputs, refs = refs[:g_out], refs[g_out:]
    (c_scr, n_scr, m_scr), (st_scr,) = refs[:3], refs[3:]
    step = pl.program_id(0)

    @pl.when((step == 0) | (step == n_reset))
    def _():
        c_scr[...] = jnp.zeros(c_scr.shape, F32)
        n_scr[...] = jnp.zeros(n_scr.shape, F32)
        m_scr[...] = jnp.full(m_scr.shape, NEG, F32)
        st_scr[...] = jnp.zeros(st_scr.shape, F32)

    m_steps = _mlstm_step(*m_inputs, *m_outputs, c_scr, n_scr, m_scr, nbatch=nbatch)
    g_steps = _gla_step(*g_inputs, *g_outputs, st_scr, nbatch=nbatch)
    for _ in itertools.zip_longest(g_steps, m_steps):
        pass


def _mixers(z, zs, zst, wlr, blr, nbatch, n_first, n_total):
    m_parts = _mlstm_call_parts(z, zst, nbatch, n_first, n_total)
    g_parts = _gla_call_parts(z, zs, wlr, blr, nbatch, n_first, n_total)
    in_specs, operands, out_specs, out_shapes, scratch = (a + b for a, b in zip(m_parts, g_parts))
    return pl.pallas_call(
        functools.partial(_mixers_kernel, nbatch=nbatch, n_reset=n_first,
                          n_in=(len(m_parts[0]), len(g_parts[0])),
                          n_out=(len(m_parts[2]), len(g_parts[2]))),
        grid=(n_total,),
        in_specs=in_specs,
        out_specs=out_specs,
        out_shape=out_shapes,
        scratch_shapes=scratch,
        compiler_params=_cparams(("arbitrary",)),
        name="mixers",
    )(*operands)


def _split_cols(t):
    o_mi = 4 * M_WIDTH
    o_gq = o_mi + 4 * M_HEADS
    o_lr = o_gq + 2 * G_KWIDTH + 2 * G_VWIDTH
    o_ga = o_lr + 2 * G_LOWRANK
    big = jnp.concatenate([t[..., o_ga:], t[..., :o_mi], t[..., o_gq:o_lr]], axis=-1)
    small = jnp.concatenate([t[..., o_mi:o_gq], t[..., o_lr:o_ga]], axis=-1)
    pad = [(0, 0)] * (t.ndim - 1) + [(0, SMALL_WIDTH - GATE_ROWS)]
    return big, jnp.pad(small, pad)


def _low_rank_weights(w_lr2):
    out = []
    for d in range(2):
        lo = 4 * M_HEADS + d * G_LOWRANK
        out.append(jnp.zeros((SMALL_WIDTH, G_KWIDTH), F32).at[lo:lo + G_LOWRANK].set(w_lr2[d])
                   .astype(BF16))
    return out


def _interleave_kernel(xs_ref, xp_ref, out_ref, *, nbatch, n_first):
    step = pl.program_id(0)

    @pl.when(step < n_first)
    def _():
        for bi in range(nbatch):
            out_ref[bi * CHUNK:(bi + 1) * CHUNK, :] = xs_ref[bi, 0]

    @pl.when(step >= n_first)
    def _():
        for bi in range(nbatch):
            out_ref[bi * CHUNK:(bi + 1) * CHUNK, :] = xp_ref[bi, 0]


def _interleave(x_first, x_second):
    nb, s1, D = x_first.shape
    s2 = x_second.shape[1]
    n_first = s1 // CHUNK
    n_second = s2 // CHUNK
    blk = (nb, 1, CHUNK, D)
    return pl.pallas_call(
        functools.partial(_interleave_kernel, nbatch=nb, n_first=n_first),
        grid=(n_first + n_second,),
        in_specs=[
            pl.BlockSpec(blk, lambda n: (0, jnp.minimum(n, n_first - 1), 0, 0)),
            pl.BlockSpec(blk, lambda n: (0, jnp.maximum(n - n_first, 0), 0, 0)),
        ],
        out_specs=pl.BlockSpec((nb * CHUNK, D), lambda n: (n, 0)),
        out_shape=jax.ShapeDtypeStruct((nb * (s1 + s2), D), x_first.dtype),
        compiler_params=_cparams(("arbitrary",)),
        name="interleave",
    )(x_first.reshape(nb, n_first, CHUNK, D), x_second.reshape(nb, n_second, CHUNK, D))


def kernel(x_prompt, x_sample, ln1, w_in, b_in, m_norm, w_lr2, b_lr2, g_norm, w_pa, w_pb, w_o,
           ln2, w_gu, w_down, ln_f):
    bp, sp, _ = x_prompt.shape
    bs, ss, _ = x_sample.shape
    assert bp == bs, "both request groups must have the same batch size"
    nb = bs
    n_first = ss // CHUNK
    n_total = n_first + sp // CHUNK
    x = _interleave(x_sample, x_prompt)
    scan_mat = _scan_matrix()

    w_big, w_small = _split_cols(w_in.astype(BF16))
    b_big, b_small = _split_cols(b_in)
    w_pa, w_pb, w_o, w_gu, w_down = (t.astype(BF16) for t in (w_pa, w_pb, w_o, w_gu, w_down))

    for l in range(ln1.shape[0]):
        wlr = _low_rank_weights(w_lr2[l])
        blr = [b_lr2[l, 0][None, :], b_lr2[l, 1][None, :]]
        z, zs, zst = _inproj(x, ln1[l][None, :], w_big, l, b_big[l][None, :], w_small[l],
                             b_small[l][None, :], scan_mat)
        af, ab, bf, bb = _mixers(z, zs, zst, wlr, blr, nb, n_first, n_total)
        x, xn = _mixout(x, af, ab, bf, bb, z, m_norm[l][None, :], g_norm[l][None, :],
                        ln2[l][None, :], w_pa, w_pb, w_o, l)
        hmid = _ffn_up(xn, w_gu, l)
        x = _resmm(x, hmid, w_down, l, name="ffn_down")

    g_f = ln_f[None, :]
    return (_final_norm(x, g_f, n_first, nb, sp), _final_norm(x, g_f, 0, nb, ss))
```
